```python
import math
import numpy as np
import jax, jax.numpy as jnp
from jax import lax

D_MODEL = 1024
BATCH = 8
SEQ = 4096
DEPTH = 4

DN_HEADS = 4
DN_HEAD_DIM = 128
DN_WIDTH = DN_HEADS * DN_HEAD_DIM
DN_CONV = 4
DN_CHUNK = 64
SWA_HEADS = 8
SWA_HEAD_DIM = 64
SWA_WIDTH = SWA_HEADS * SWA_HEAD_DIM
DILATED_CONFIGS = ((128, 1), (512, 4), (2048, 16))
ATTN_BLOCK = 128
ROPE_THETA = 500000.0
ROPE_DIM = SWA_HEAD_DIM // 4
D_MIX = DN_WIDTH + SWA_WIDTH
IN_SPLITS = (3 * DN_WIDTH, DN_WIDTH, DN_HEADS, DN_HEADS, SWA_WIDTH, SWA_WIDTH, SWA_WIDTH)
IN_WIDTH = sum(IN_SPLITS)
D_FF = 2816
FFN_CONV = 3
NORM_EPS = 1e-6

kernel_name = "hymba_style_deltanet_dilated_swa_convglu"


def _rmsnorm(x, w):
    xf = x.astype(jnp.float32)
    y = xf * lax.rsqrt(jnp.mean(xf * xf, axis=-1, keepdims=True) + NORM_EPS)
    return (y * w.astype(jnp.float32)).astype(x.dtype)


def _l2norm(x):
    xf = x.astype(jnp.float32)
    return xf * lax.rsqrt(jnp.sum(xf * xf, axis=-1, keepdims=True) + NORM_EPS)


def _causal_dwconv(x, w):
    K = w.shape[0]
    S = x.shape[1]
    xp = jnp.pad(x, ((0, 0), (K - 1, 0), (0, 0)))
    y = xp[:, 0:S] * w[0]
    for j in range(1, K):
        y = y + xp[:, j:j + S] * w[j]
    return y


def _rope_tables(S):
    pos = jnp.arange(S, dtype=jnp.float32)
    inv_freq = ROPE_THETA ** (-jnp.arange(0, ROPE_DIM, 2, dtype=jnp.float32) / ROPE_DIM)
    ang = pos[:, None] * inv_freq[None, :]
    return jnp.cos(ang), jnp.sin(ang)


def _partial_rope(x, cos, sin):
    half = ROPE_DIM // 2
    c = cos[None, :, None, :].astype(x.dtype)
    s = sin[None, :, None, :].astype(x.dtype)
    x1, x2, rest = x[..., :half], x[..., half:ROPE_DIM], x[..., ROPE_DIM:]
    return jnp.concatenate([x1 * c - x2 * s, x2 * c + x1 * s, rest], axis=-1)


def _gated_delta_rule(q, k, v, g, beta):
    B, S, H, Dk = q.shape
    Dv = v.shape[-1]
    C = DN_CHUNK
    n = S // C
    f32 = jnp.float32

    def chunks(t):
        return t.astype(f32).reshape(B, n, C, H, -1).transpose(0, 3, 1, 2, 4)

    q = chunks(q) * (Dk ** -0.5)
    k = chunks(k)
    v = chunks(v)
    g = g.astype(f32).reshape(B, n, C, H).transpose(0, 3, 1, 2)
    beta = beta.astype(f32).reshape(B, n, C, H).transpose(0, 3, 1, 2)
    G = jnp.cumsum(g, axis=-1)
    idx = jnp.arange(C)
    incl = idx[:, None] >= idx[None, :]
    strict = idx[:, None] > idx[None, :]
    gamma = jnp.exp(jnp.where(incl, G[..., :, None] - G[..., None, :], -jnp.inf))
    kb = k * beta[..., None]
    a_kk = jnp.where(strict, jnp.einsum('bhnid,bhnjd->bhnij', kb, k) * gamma, 0.0)
    tri = a_kk + jnp.eye(C, dtype=f32)
    rhs = jnp.concatenate([v * beta[..., None], kb * jnp.exp(G)[..., None]], axis=-1)
    sol = lax.linalg.triangular_solve(tri, rhs, left_side=True, lower=True, unit_diagonal=True)
    u, w = sol[..., :Dv], sol[..., Dv:]
    qk = jnp.einsum('bhnid,bhnjd->bhnij', q, k) * gamma
    qg = q * jnp.exp(G)[..., None]
    kd = k * jnp.exp(G[..., -1:] - G)[..., None]
    dc = jnp.exp(G[..., -1])

    def step(state, xs):
        u_c, w_c, qk_c, qg_c, kd_c, dc_c = xs
        v_new = u_c - jnp.einsum('bhcd,bhde->bhce', w_c, state)
        o_c = jnp.einsum('bhcd,bhde->bhce', qg_c, state) + jnp.einsum('bhij,bhje->bhie', qk_c, v_new)
        state = state * dc_c[..., None, None] + jnp.einsum('bhcd,bhce->bhde', kd_c, v_new)
        return state, o_c

    xs = tuple(jnp.moveaxis(t, 2, 0) for t in (u, w, qk, qg, kd, dc))
    state0 = jnp.zeros((B, H, Dk, Dv), f32)
    _, o = lax.scan(step, state0, xs)
    return o.transpose(1, 0, 3, 2, 4).reshape(B, S, H, Dv)


def _causal_window_attention(q, k, v, span):
    N, L, H, Dh = q.shape
    nb = -(-L // ATTN_BLOCK)
    pad = nb * ATTN_BLOCK - L
    qb = jnp.pad(q, ((0, 0), (0, pad), (0, 0), (0, 0))).reshape(N, nb, ATTN_BLOCK, H, Dh)

    def kv_blocks(t):
        t = jnp.pad(t, ((0, 0), (ATTN_BLOCK, pad), (0, 0), (0, 0))).reshape(N, nb + 1, ATTN_BLOCK, H, Dh)
        return jnp.concatenate([t[:, :-1], t[:, 1:]], axis=2)

    kw, vw = kv_blocks(k), kv_blocks(v)
    s = jnp.einsum('nbqhd,nbkhd->nbhqk', qb, kw, preferred_element_type=jnp.float32)
    qi = jnp.arange(ATTN_BLOCK)[:, None]
    ki = jnp.arange(2 * ATTN_BLOCK)[None, :]
    dist = qi + ATTN_BLOCK - ki
    key_pos = jnp.arange(nb)[:, None, None] * ATTN_BLOCK - ATTN_BLOCK + ki[None]
    valid = (dist >= 0) & (dist <= span) & (key_pos >= 0)
    s = jnp.where(valid[None, :, None], s, -jnp.inf)
    lse = jax.nn.logsumexp(s, axis=-1)
    p = jnp.exp(s - lse[..., None])
    o = jnp.einsum('nbhqk,nbkhd->nbqhd', p, vw.astype(jnp.float32))
    o = o.reshape(N, nb * ATTN_BLOCK, H, Dh)[:, :L]
    lse = lse.transpose(0, 1, 3, 2).reshape(N, nb * ATTN_BLOCK, H)[:, :L]
    return o, lse


def _dilated_attention(q, k, v):
    B, S, H, Dh = q.shape
    q = q * (Dh ** -0.5)
    outs, lses = [], []
    for window, dil in DILATED_CONFIGS:
        L = S // dil

        def by_residue(t):
            return t.reshape(B, L, dil, H, Dh).transpose(0, 2, 1, 3, 4).reshape(B * dil, L, H, Dh)

        o, lse = _causal_window_attention(by_residue(q), by_residue(k), by_residue(v), window // dil)
        outs.append(o.reshape(B, dil, L, H, Dh).transpose(0, 2, 1, 3, 4).reshape(B, S, H, Dh))
        lses.append(lse.reshape(B, dil, L, H).transpose(0, 2, 1, 3).reshape(B, S, H))
    wts = jax.nn.softmax(jnp.stack(lses), axis=0)
    o = jnp.einsum('gbsh,gbshd->bshd', wts, jnp.stack(outs))
    return o.astype(v.dtype)


def _hybrid_mixer(h, cos, sin, w_in, dn_conv, dn_a_log, dn_dt_bias, dn_out_norm, w_out):
    B, S, _ = h.shape
    proj = h @ w_in
    a_qkv, a_z, a_b, a_a, b_q, b_k, b_v = jnp.split(proj, np.cumsum(IN_SPLITS)[:-1], axis=-1)
    a_qkv = jax.nn.silu(_causal_dwconv(a_qkv, dn_conv))
    aq, ak, av = jnp.split(a_qkv, 3, axis=-1)
    heads_a = lambda t: t.reshape(B, S, DN_HEADS, DN_HEAD_DIM)
    aq, ak, av = _l2norm(heads_a(aq)), _l2norm(heads_a(ak)), heads_a(av)
    beta = jax.nn.sigmoid(a_b.astype(jnp.float32))
    g = -jnp.exp(dn_a_log.astype(jnp.float32)) * jax.nn.softplus(a_a.astype(jnp.float32) + dn_dt_bias.astype(jnp.float32))
    o_a = _gated_delta_rule(aq, ak, av, g, beta).astype(h.dtype)
    o_a = _rmsnorm(o_a, dn_out_norm) * jax.nn.silu(heads_a(a_z))
    heads_b = lambda t: t.reshape(B, S, SWA_HEADS, SWA_HEAD_DIM)
    bq = _partial_rope(heads_b(b_q), cos, sin)
    bk = _partial_rope(heads_b(b_k), cos, sin)
    o_b = _dilated_attention(bq, bk, heads_b(b_v))
    mixed = jnp.concatenate([o_a.reshape(B, S, DN_WIDTH), o_b.reshape(B, S, SWA_WIDTH)], axis=-1)
    return mixed @ w_out


def _conv_glu_ffn(h, ffn_up, ffn_conv, ffn_conv_bias, ffn_down):
    u = _causal_dwconv(h @ ffn_up, ffn_conv) + ffn_conv_bias
    gate, val = jnp.split(u, 2, axis=-1)
    return (jax.nn.gelu(gate, approximate=True) * val) @ ffn_down


def setup_inputs(seed: int = 0) -> dict:
    key = jax.random.key(seed)
    ks = jax.random.split(key, 16)
    f32 = jnp.float32

    def normal(k, shape, scale):
        return jax.random.normal(k, shape, f32) * scale

    def gain(k, shape):
        return 1.0 + 0.02 * jax.random.normal(k, shape, f32)

    x = normal(ks[0], (BATCH, SEQ, D_MODEL), 1.0)
    pre_mix_norm = gain(ks[1], (DEPTH, D_MODEL))
    w_in = normal(ks[2], (DEPTH, D_MODEL, IN_WIDTH), D_MODEL ** -0.5)
    dn_conv = normal(ks[3], (DEPTH, DN_CONV, 3 * DN_WIDTH), DN_CONV ** -0.5)
    dn_a_log = jnp.log(jax.random.uniform(ks[4], (DEPTH, DN_HEADS), f32, 1.0, 16.0))
    dt = jnp.exp(jax.random.uniform(ks[5], (DEPTH, DN_HEADS), f32, math.log(1e-3), math.log(1e-1)))
    dn_dt_bias = dt + jnp.log(-jnp.expm1(-dt))
    dn_out_norm = gain(ks[6], (DEPTH, DN_HEAD_DIM))
    w_out = normal(ks[7], (DEPTH, D_MIX, D_MODEL), D_MIX ** -0.5)
    post_mix_norm = gain(ks[8], (DEPTH, D_MODEL))
    pre_ffn_norm = gain(ks[9], (DEPTH, D_MODEL))
    ffn_up = normal(ks[10], (DEPTH, D_MODEL, 2 * D_FF), D_MODEL ** -0.5)
    ffn_conv = normal(ks[11], (DEPTH, FFN_CONV, 2 * D_FF), FFN_CONV ** -0.5)
    ffn_conv_bias = normal(ks[12], (DEPTH, 2 * D_FF), 0.02)
    ffn_down = normal(ks[13], (DEPTH, D_FF, D_MODEL), D_FF ** -0.5)
    post_ffn_norm = gain(ks[14], (DEPTH, D_MODEL))
    return {"x": x, "pre_mix_norm": pre_mix_norm, "w_in": w_in, "dn_conv": dn_conv,
            "dn_a_log": dn_a_log, "dn_dt_bias": dn_dt_bias, "dn_out_norm": dn_out_norm,
            "w_out": w_out, "post_mix_norm": post_mix_norm, "pre_ffn_norm": pre_ffn_norm,
            "ffn_up": ffn_up, "ffn_conv": ffn_conv, "ffn_conv_bias": ffn_conv_bias,
            "ffn_down": ffn_down, "post_ffn_norm": post_ffn_norm}


def reference(x, pre_mix_norm, w_in, dn_conv, dn_a_log, dn_dt_bias, dn_out_norm, w_out,
              post_mix_norm, pre_ffn_norm, ffn_up, ffn_conv, ffn_conv_bias, ffn_down, post_ffn_norm):
    S = x.shape[1]
    cos, sin = _rope_tables(S)
    for l in range(DEPTH):
        h = _rmsnorm(x, pre_mix_norm[l])
        h = _hybrid_mixer(h, cos, sin, w_in[l], dn_conv[l], dn_a_log[l], dn_dt_bias[l], dn_out_norm[l], w_out[l])
        x = x + _rmsnorm(h, post_mix_norm[l])
        h = _rmsnorm(x, pre_ffn_norm[l])
        h = _conv_glu_ffn(h, ffn_up[l], ffn_conv[l], ffn_conv_bias[l], ffn_down[l])
        x = x + _rmsnorm(h, post_ffn_norm[l])
    return x
```

```python
import functools
import math

import jax
import jax.numpy as jnp
from jax import lax
from jax.experimental import pallas as pl
from jax.experimental.pallas import tpu as pltpu

F32 = jnp.float32
BF16 = jnp.bfloat16

DN_HEADS = 4
DN_HEAD_DIM = 128
DN_WIDTH = DN_HEADS * DN_HEAD_DIM
DN_CONV = 4
SWA_HEADS = 8
SWA_HEAD_DIM = 64
SWA_WIDTH = SWA_HEADS * SWA_HEAD_DIM
DILATIONS = (1, 4, 16)
ATTN_BLOCK = 128
ATTN_SPAN = 128
ROPE_THETA = 500000.0
ROPE_DIM = SWA_HEAD_DIM // 4
ROPE_HALF = ROPE_DIM // 2
FFN_CONV = 3
NORM_EPS = 1e-6

LANES = 128
SUBLANES = 8
BF16_ROWS = 16
VMEM_LIMIT_BYTES = 56 * 1024 * 1024

TOKEN_TILE = 512
DELTA_BLOCK = 256
DELTA_CHUNK = 128
INV_BASE = 8
PROJ_TILE = 512
N_PROJ_TILES = 7
GATE_LANES = 2 * LANES


def _params(n_axes):
    return pltpu.CompilerParams(
        dimension_semantics=("arbitrary",) * n_axes, vmem_limit_bytes=VMEM_LIMIT_BYTES)


def _sigmoid(x):
    return 1.0 / (1.0 + jnp.exp(-x))


def _rms_scale(x):
    return lax.rsqrt(jnp.mean(x * x, axis=-1, keepdims=True) + NORM_EPS)


def _split3(x):
    hi = x.astype(BF16)
    r = x - hi.astype(F32)
    mid = r.astype(BF16)
    lo = (r - mid.astype(F32)).astype(BF16)
    return hi, mid, lo


def _in_proj_kernel(x_ref, nw_ref, w_ref, wg_ref, cos_ref, sa_ref, sb_ref, proj_ref, gate_ref):
    x = x_ref[...]
    h = (x * _rms_scale(x) * nw_ref[...]).astype(BF16)
    gate_ref[...] = jnp.dot(h, wg_ref[...], preferred_element_type=F32)
    for j in range(N_PROJ_TILES):
        cols = slice(j * PROJ_TILE, (j + 1) * PROJ_TILE)
        y = jnp.dot(h, w_ref[:, cols], preferred_element_type=F32)
        if j in (4, 5):
            if j == 4:
                y = y * (SWA_HEAD_DIM ** -0.5)
            y = (y * cos_ref[...]
                 + pltpu.roll(y, PROJ_TILE - ROPE_HALF, 1) * sa_ref[...]
                 + pltpu.roll(y, ROPE_HALF, 1) * sb_ref[...])
        proj_ref[:, cols] = y.astype(BF16)


def _in_proj(x, norm_w, w_main, w_gate, cos_f, sin_a, sin_b):
    B, S, D = x.shape
    tm = min(TOKEN_TILE, S)
    n_cols = N_PROJ_TILES * PROJ_TILE
    return pl.pallas_call(
        _in_proj_kernel,
        grid=(S // tm, B),
        in_specs=[
            pl.BlockSpec((None, tm, D), lambda s, b: (b, s, 0)),
            pl.BlockSpec((1, D), lambda s, b: (0, 0)),
            pl.BlockSpec((D, n_cols), lambda s, b: (0, 0)),
            pl.BlockSpec((D, GATE_LANES), lambda s, b: (0, 0)),
            pl.BlockSpec((tm, PROJ_TILE), lambda s, b: (s, 0)),
            pl.BlockSpec((tm, PROJ_TILE), lambda s, b: (s, 0)),
            pl.BlockSpec((tm, PROJ_TILE), lambda s, b: (s, 0)),
        ],
        out_specs=[
            pl.BlockSpec((None, tm, n_cols), lambda s, b: (b, s, 0)),
            pl.BlockSpec((None, tm, GATE_LANES), lambda s, b: (b, s, 0)),
        ],
        out_shape=[
            jax.ShapeDtypeStruct((B, S, n_cols), BF16),
            jax.ShapeDtypeStruct((B, S, GATE_LANES), F32),
        ],
        compiler_params=_params(2),
        name="in_proj",
    )(x, norm_w, w_main, w_gate, cos_f, sin_a, sin_b)


def _delta_kernel(qkv_ref, z_ref, gate_ref, cw_ref, alog_ref, dtb_ref, onw_ref, o_ref,
                  xs_scr, state_scr, *, tb, chunk):
    halo = SUBLANES

    @pl.when(pl.program_id(1) == 0)
    def _():
        xs_scr[0:halo, :] = jnp.zeros((halo, 3 * DN_WIDTH), F32)
        state_scr[...] = jnp.zeros_like(state_scr)

    @pl.when(pl.program_id(1) > 0)
    def _():
        xs_scr[0:halo, :] = xs_scr[tb:tb + halo, :]

    xs_scr[halo:halo + tb, :] = qkv_ref[...].astype(F32)

    def conv_silu(col0):
        cols = slice(col0, col0 + DN_HEAD_DIM)
        acc = xs_scr[halo:halo + tb, cols] * cw_ref[DN_CONV - 1:DN_CONV, cols]
        for j in range(DN_CONV - 1):
            off = halo - (DN_CONV - 1) + j
            acc = acc + xs_scr[off:off + tb, cols] * cw_ref[j:j + 1, cols]
        return acc * _sigmoid(acc)

    def l2norm(t):
        return t * lax.rsqrt(jnp.sum(t * t, axis=-1, keepdims=True) + NORM_EPS)

    gates = gate_ref[...]
    beta_all = _sigmoid(gates[:, :LANES])
    a_in = gates[:, LANES:] + dtb_ref[...]
    softplus = jnp.maximum(a_in, 0.0) + jnp.log(1.0 + jnp.exp(-jnp.abs(a_in)))
    g_all = -jnp.exp(alog_ref[...]) * softplus
    g_parts = _split3(g_all)

    r_i = lax.broadcasted_iota(jnp.int32, (chunk, chunk), 0)
    c_i = lax.broadcasted_iota(jnp.int32, (chunk, chunk), 1)
    incl = r_i >= c_i
    strict = r_i > c_i
    tri = jnp.where(incl, 1.0, 0.0).astype(BF16)
    eye = jnp.where(r_i == c_i, 1.0, 0.0).astype(F32)

    def same_block(log_size):
        return (r_i >> log_size) == (c_i >> log_size)

    log_s = int(math.log2(INV_BASE))
    level_masks = [same_block(log_s)]
    while (1 << log_s) < chunk:
        level_masks.append(same_block(log_s + 1) & (((r_i >> log_s) & 1) == 1)
                           & (((c_i >> log_s) & 1) == 0))
        log_s += 1

    qs, ks, vs = [], [], []
    for h in range(DN_HEADS):
        qs.append(l2norm(conv_silu(h * DN_HEAD_DIM)) * (DN_HEAD_DIM ** -0.5))
        ks.append(l2norm(conv_silu(DN_WIDTH + h * DN_HEAD_DIM)))
        vs.append(conv_silu(2 * DN_WIDTH + h * DN_HEAD_DIM))

    states = [state_scr[h] for h in range(DN_HEADS)]
    for c in range(tb // chunk):
        rows = slice(c * chunk, (c + 1) * chunk)
        G = sum(jnp.dot(tri, p[rows], preferred_element_type=F32) for p in g_parts)
        G_t = G.T
        for h in range(DN_HEADS):
            q, k, v = qs[h][rows], ks[h][rows], vs[h][rows]
            g_col = G[:, h:h + 1]
            g_row = G_t[h:h + 1, :]
            g_last = g_row[:, chunk - 1:chunk]
            b_col = beta_all[rows, h:h + 1]
            gamma = jnp.exp(jnp.where(incl, g_col - g_row, -jnp.inf))
            e_g = jnp.exp(g_col)
            kb = k * b_col
            kq = jnp.concatenate([kb, q], axis=0).astype(BF16)
            kk_qk = lax.dot_general(kq, k.astype(BF16), (((1,), (1,)), ((), ())),
                                    preferred_element_type=F32)
            a_mat = jnp.where(strict, kk_qk[:chunk] * gamma, 0.0)
            qk = kk_qk[chunk:] * gamma
            x_mat = jnp.where(level_masks[0], -a_mat, 0.0)
            t_inv = eye + x_mat
            p_pow = x_mat
            for _ in range(int(math.log2(INV_BASE)) - 1):
                p_pow = jnp.dot(p_pow.astype(BF16), p_pow.astype(BF16), preferred_element_type=F32)
                t_inv = t_inv + jnp.dot(p_pow.astype(BF16), t_inv.astype(BF16),
                                        preferred_element_type=F32)
            for lvl_mask in level_masks[1:]:
                t_b = t_inv.astype(BF16)
                ld = jnp.dot(jnp.where(lvl_mask, a_mat, 0.0).astype(BF16), t_b,
                             preferred_element_type=F32)
                t_inv = t_inv - jnp.dot(t_b, ld.astype(BF16), preferred_element_type=F32)
            rhs = jnp.concatenate([v * b_col, kb * e_g], axis=1).astype(BF16)
            sol = jnp.dot(t_inv.astype(BF16), rhs, preferred_element_type=F32)
            u, w = sol[:, :DN_HEAD_DIM], sol[:, DN_HEAD_DIM:]
            state = states[h]
            wq = jnp.concatenate([w, q * e_g], axis=0).astype(BF16)
            ws = jnp.dot(wq, state.astype(BF16), preferred_element_type=F32)
            v_new = u - ws[:chunk]
            v_new_b = v_new.astype(BF16)
            o = ws[chunk:] + jnp.dot(qk.astype(BF16), v_new_b, preferred_element_type=F32)
            kd_t = (k * jnp.exp(g_last - g_col)).T.astype(BF16)
            states[h] = state * jnp.exp(g_last) + jnp.dot(kd_t, v_new_b, preferred_element_type=F32)
            cols = slice(h * DN_HEAD_DIM, (h + 1) * DN_HEAD_DIM)
            zf = z_ref[rows, cols].astype(F32)
            o = o * _rms_scale(o) * onw_ref[...] * (zf * _sigmoid(zf))
            o_ref[rows, cols] = o.astype(BF16)
    for h in range(DN_HEADS):
        state_scr[h] = states[h]


def _delta_mixer(proj, gates, conv_w, a_log, dt_bias, out_norm_w):
    B, S, _ = proj.shape
    tb = min(DELTA_BLOCK, S)
    chunk = min(DELTA_CHUNK, tb)
    kern = functools.partial(_delta_kernel, tb=tb, chunk=chunk)
    return pl.pallas_call(
        kern,
        grid=(B, S // tb),
        in_specs=[
            pl.BlockSpec((None, tb, 3 * DN_WIDTH), lambda b, s: (b, s, 0)),
            pl.BlockSpec((None, tb, DN_WIDTH), lambda b, s: (b, s, 3)),
            pl.BlockSpec((None, tb, GATE_LANES), lambda b, s: (b, s, 0)),
            pl.BlockSpec((DN_CONV, 3 * DN_WIDTH), lambda b, s: (0, 0)),
            pl.BlockSpec((1, LANES), lambda b, s: (0, 0)),
            pl.BlockSpec((1, LANES), lambda b, s: (0, 0)),
            pl.BlockSpec((1, DN_HEAD_DIM), lambda b, s: (0, 0)),
        ],
        out_specs=pl.BlockSpec((None, tb, DN_WIDTH), lambda b, s: (b, s, 0)),
        out_shape=jax.ShapeDtypeStruct((B, S, DN_WIDTH), BF16),
        scratch_shapes=[
            pltpu.VMEM((tb + SUBLANES, 3 * DN_WIDTH), F32),
            pltpu.VMEM((DN_HEADS, DN_HEAD_DIM, DN_HEAD_DIM), F32),
        ],
        compiler_params=_params(2),
        name="delta_mixer",
    )(proj, proj, gates, conv_w, a_log, dt_bias, out_norm_w)


def _attn_kernel(q_ref, kp_ref, kc_ref, vp_ref, vc_ref, o_ref, lse_ref, *, bq):
    blk = ATTN_BLOCK
    q_i = lax.broadcasted_iota(jnp.int32, (blk, 2 * blk), 0)
    k_i = lax.broadcasted_iota(jnp.int32, (blk, 2 * blk), 1)
    band = (k_i >= q_i) & (k_i <= q_i + ATTN_SPAN)
    first_lo = jnp.where(pl.program_id(2) == 0, blk, 0)
    band_first = band & (k_i >= first_lo)
    lane = lax.broadcasted_iota(jnp.int32, (blk, LANES), 1)
    low_half = lane < SWA_HEAD_DIM

    for s in range(bq // blk):
        rows = slice(s * blk, (s + 1) * blk)
        q = q_ref[rows, :]
        if s == 0:
            k = jnp.concatenate([kp_ref[...], kc_ref[0:blk, :]], axis=0)
            v = jnp.concatenate([vp_ref[...], vc_ref[0:blk, :]], axis=0)
            valid = band_first
        else:
            k = kc_ref[(s - 1) * blk:(s + 1) * blk, :]
            v = vc_ref[(s - 1) * blk:(s + 1) * blk, :]
            valid = band
        lse_acc = jnp.zeros((blk, LANES), F32)
        for pair in range(SWA_HEADS // 2):
            cols = slice(pair * LANES, (pair + 1) * LANES)
            q2, k2, v2 = q[:, cols], k[:, cols], v[:, cols]
            halves = []
            for half in range(2):
                mask = low_half if half == 0 else jnp.logical_not(low_half)
                qh = jnp.where(mask, q2, jnp.zeros_like(q2))
                sc = lax.dot_general(qh, k2, (((1,), (1,)), ((), ())), preferred_element_type=F32)
                sc = jnp.where(valid, sc, -jnp.inf)
                m = jnp.max(sc, axis=-1, keepdims=True)
                p = jnp.exp(sc - m)
                l = jnp.sum(p, axis=-1, keepdims=True)
                halves.append(jnp.dot(p.astype(BF16), v2, preferred_element_type=F32) / l)
                lse_acc = jnp.where(lane == 2 * pair + half, m + jnp.log(l), lse_acc)
            o_ref[rows, cols] = jnp.where(low_half, halves[0], halves[1]).astype(BF16)
        lse_ref[rows, :] = lse_acc


def _dilated_attention_pass(proj, dil):
    B, S, n_cols = proj.shape
    L = S // dil
    bq = min(512, L)
    blk = ATTN_BLOCK
    view = proj.reshape(B, L, dil * n_cols)
    tiles = N_PROJ_TILES
    sub = bq // blk
    kern = functools.partial(_attn_kernel, bq=bq)

    def cur(col):
        return pl.BlockSpec((None, bq, PROJ_TILE), lambda b, r, i: (b, i, r * tiles + col))

    def prev(col):
        return pl.BlockSpec((None, blk, PROJ_TILE),
                            lambda b, r, i: (b, jnp.maximum(i * sub - 1, 0), r * tiles + col))

    o, lse = pl.pallas_call(
        kern,
        grid=(B, dil, L // bq),
        in_specs=[cur(4), prev(5), cur(5), prev(6), cur(6)],
        out_specs=[
            pl.BlockSpec((None, bq, SWA_WIDTH), lambda b, r, i: (b, i, r)),
            pl.BlockSpec((None, bq, LANES), lambda b, r, i: (b, i, r)),
        ],
        out_shape=[
            jax.ShapeDtypeStruct((B, L, dil * SWA_WIDTH), BF16),
            jax.ShapeDtypeStruct((B, L, dil * LANES), F32),
        ],
        compiler_params=_params(3),
        name=f"dilated_attn_d{dil}",
    )(view, view, view, view, view)
    return o.reshape(B, S, SWA_WIDTH), lse.reshape(B, S, LANES)


def _out_proj_kernel(oa_ref, o1_ref, o2_ref, o3_ref, l1_ref, l2_ref, l3_ref, x_ref, w_ref,
                     e_ref, nw_ref, out_ref):
    lses = [l1_ref[...], l2_ref[...], l3_ref[...]]
    m = jnp.maximum(jnp.maximum(lses[0], lses[1]), lses[2])
    es = [jnp.exp(l - m) for l in lses]
    den = es[0] + es[1] + es[2]

    def expand(wt):
        hi = wt.astype(BF16)
        lo = (wt - hi.astype(F32)).astype(BF16)
        return (jnp.dot(hi, e_ref[...], preferred_element_type=F32)
                + jnp.dot(lo, e_ref[...], preferred_element_type=F32))

    ob = None
    for e, o_ref in zip(es, (o1_ref, o2_ref, o3_ref)):
        term = expand(e / den) * o_ref[...].astype(F32)
        ob = term if ob is None else ob + term
    y = (jnp.dot(oa_ref[...], w_ref[0:DN_WIDTH, :], preferred_element_type=F32)
         + jnp.dot(ob.astype(BF16), w_ref[DN_WIDTH:, :], preferred_element_type=F32))
    out_ref[...] = x_ref[...] + y * _rms_scale(y) * nw_ref[...]


def _out_proj(o_a, outs, lses, x, w_out, expand_mat, norm_w):
    B, S, D = x.shape
    tm = min(TOKEN_TILE, S)

    def tok(width):
        return pl.BlockSpec((None, tm, width), lambda b, s: (b, s, 0))

    return pl.pallas_call(
        _out_proj_kernel,
        grid=(B, S // tm),
        in_specs=[tok(DN_WIDTH), tok(SWA_WIDTH), tok(SWA_WIDTH), tok(SWA_WIDTH),
                  tok(LANES), tok(LANES), tok(LANES), tok(D),
                  pl.BlockSpec((DN_WIDTH + SWA_WIDTH, D), lambda b, s: (0, 0)),
                  pl.BlockSpec((LANES, SWA_WIDTH), lambda b, s: (0, 0)),
                  pl.BlockSpec((1, D), lambda b, s: (0, 0))],
        out_specs=tok(D),
        out_shape=jax.ShapeDtypeStruct((B, S, D), F32),
        compiler_params=_params(2),
        name="out_proj",
    )(o_a, *outs, *lses, x, w_out, expand_mat, norm_w)


def _ffn_kernel(x_ref, xh_ref, nw_ref, wg_ref, wv_ref, cg_ref, cv_ref, bg_ref, bv_ref, wd_ref,
                pnw_ref, out_ref, h_scr, u_scr, acc_scr, *, tm, tiles_per_seq):
    halo = BF16_ROWS
    j = pl.program_id(1)

    @pl.when(j == 0)
    def _():
        x = x_ref[...]
        h_scr[halo:, :] = (x * _rms_scale(x) * nw_ref[...]).astype(BF16)
        xh = xh_ref[...]
        hh = xh * _rms_scale(xh) * nw_ref[...]
        starts_seq = (pl.program_id(0) % tiles_per_seq) == 0
        h_scr[0:halo, :] = jnp.where(starts_seq, 0.0, hh).astype(BF16)
        acc_scr[...] = jnp.zeros_like(acc_scr)

    def conv(w_ref, c_ref, b_ref):
        u_scr[...] = jnp.dot(h_scr[...], w_ref[...], preferred_element_type=F32)
        acc = u_scr[halo:halo + tm, :] * c_ref[FFN_CONV - 1:FFN_CONV, :]
        for t in range(FFN_CONV - 1):
            off = halo - (FFN_CONV - 1) + t
            acc = acc + u_scr[off:off + tm, :] * c_ref[t:t + 1, :]
        return acc + b_ref[...]

    gate = conv(wg_ref, cg_ref, bg_ref)
    cdf = 0.5 * (1.0 + jnp.tanh(math.sqrt(2.0 / math.pi) * (gate + 0.044715 * (gate * gate * gate))))
    act = (gate * cdf) * conv(wv_ref, cv_ref, bv_ref)
    acc_scr[...] += jnp.dot(act.astype(BF16), wd_ref[...], preferred_element_type=F32)

    @pl.when(j == pl.num_programs(1) - 1)
    def _():
        y = acc_scr[...]
        out_ref[...] = x_ref[...] + y * _rms_scale(y) * pnw_ref[...]


def _ffn(x, pre_w, w_up, conv_w, conv_b, w_down, post_w):
    B, S, D = x.shape
    T = B * S
    d_ff = w_down.shape[0]
    n_chunks = 2
    fc = d_ff // n_chunks
    tm = min(TOKEN_TILE, S)
    halo = BF16_ROWS
    hb = tm // halo
    xf = x.reshape(T, D)
    kern = functools.partial(_ffn_kernel, tm=tm, tiles_per_seq=S // tm)
    out = pl.pallas_call(
        kern,
        grid=(T // tm, n_chunks),
        in_specs=[
            pl.BlockSpec((tm, D), lambda i, j: (i, 0)),
            pl.BlockSpec((halo, D), lambda i, j: (jnp.maximum(i * hb - 1, 0), 0)),
            pl.BlockSpec((1, D), lambda i, j: (0, 0)),
            pl.BlockSpec((D, fc), lambda i, j: (0, j)),
            pl.BlockSpec((D, fc), lambda i, j: (0, n_chunks + j)),
            pl.BlockSpec((FFN_CONV, fc), lambda i, j: (0, j)),
            pl.BlockSpec((FFN_CONV, fc), lambda i, j: (0, n_chunks + j)),
            pl.BlockSpec((1, fc), lambda i, j: (0, j)),
            pl.BlockSpec((1, fc), lambda i, j: (0, n_chunks + j)),
            pl.BlockSpec((fc, D), lambda i, j: (j, 0)),
            pl.BlockSpec((1, D), lambda i, j: (0, 0)),
        ],
        out_specs=pl.BlockSpec((tm, D), lambda i, j: (i, 0)),
        out_shape=jax.ShapeDtypeStruct((T, D), F32),
        scratch_shapes=[
            pltpu.VMEM((tm + halo, D), BF16),
            pltpu.VMEM((tm + halo, fc), F32),
            pltpu.VMEM((tm, D), F32),
        ],
        compiler_params=_params(2),
        name="conv_glu_ffn",
    )(xf, xf, pre_w, w_up, w_up, conv_w, conv_w, conv_b, conv_b, w_down, post_w)
    return out.reshape(B, S, D)


def _rope_tables(S):
    pos = jnp.arange(S, dtype=F32)
    inv_freq = ROPE_THETA ** (-jnp.arange(0, ROPE_DIM, 2, dtype=F32) / ROPE_DIM)
    ang = pos[:, None] * inv_freq[None, :]
    cos, sin = jnp.cos(ang), jnp.sin(ang)
    pad = SWA_HEAD_DIM - ROPE_DIM
    ones = jnp.ones((S, pad), F32)
    zeros_h = jnp.zeros((S, ROPE_HALF), F32)
    zeros_p = jnp.zeros((S, pad), F32)
    cos_head = jnp.concatenate([cos, cos, ones], axis=1)
    sa_head = jnp.concatenate([-sin, zeros_h, zeros_p], axis=1)
    sb_head = jnp.concatenate([zeros_h, sin, zeros_p], axis=1)
    tile = lambda t: jnp.tile(t, (1, SWA_HEADS))
    return tile(cos_head), tile(sa_head), tile(sb_head)


def _pad_lanes(v):
    return jnp.zeros((1, LANES), F32).at[0, :v.shape[0]].set(v.astype(F32))


def kernel(x, pre_mix_norm, w_in, dn_conv, dn_a_log, dn_dt_bias, dn_out_norm, w_out, post_mix_norm,
           pre_ffn_norm, ffn_up, ffn_conv, ffn_conv_bias, ffn_down, post_ffn_norm):
    B, S, D = x.shape
    depth = w_in.shape[0]
    assert S % (max(DILATIONS) * ATTN_BLOCK) == 0 and S % TOKEN_TILE == 0
    cos_f, sin_a, sin_b = _rope_tables(S)
    qkvz = 4 * DN_WIDTH
    n_gate = 2 * DN_HEADS
    head_of_lane = jnp.arange(SWA_WIDTH) // SWA_HEAD_DIM
    expand_mat = (jnp.arange(LANES)[:, None] == head_of_lane[None, :]).astype(BF16)
    row = lambda v: v.reshape(1, -1).astype(F32)

    for l in range(depth):
        w_l = w_in[l]
        w_main = jnp.concatenate([w_l[:, :qkvz], w_l[:, qkvz + n_gate:]], axis=1).astype(BF16)
        w_gate = jnp.zeros((D, GATE_LANES), F32)
        w_gate = w_gate.at[:, :DN_HEADS].set(w_l[:, qkvz:qkvz + DN_HEADS])
        w_gate = w_gate.at[:, LANES:LANES + DN_HEADS].set(w_l[:, qkvz + DN_HEADS:qkvz + n_gate])
        proj, gates = _in_proj(x, row(pre_mix_norm[l]), w_main, w_gate.astype(BF16), cos_f, sin_a, sin_b)
        o_a = _delta_mixer(proj, gates, dn_conv[l].astype(F32), _pad_lanes(dn_a_log[l]),
                           _pad_lanes(dn_dt_bias[l]), row(dn_out_norm[l]))
        outs, lses = zip(*[_dilated_attention_pass(proj, d) for d in DILATIONS])
        x = _out_proj(o_a, outs, lses, x, w_out[l].astype(BF16), expand_mat, row(post_mix_norm[l]))
        x = _ffn(x, row(pre_ffn_norm[l]), ffn_up[l].astype(BF16), ffn_conv[l].astype(F32),
                 row(ffn_conv_bias[l]), ffn_down[l].astype(BF16), row(post_ffn_norm[l]))
    return x
```

```python
import functools
import math

import jax
import jax.numpy as jnp
from jax import lax
from jax.experimental import pallas as pl
from jax.experimental.pallas import tpu as pltpu

F32 = jnp.float32
BF16 = jnp.bfloat16

DN_HEADS = 4
DN_HEAD_DIM = 128
DN_WIDTH = DN_HEADS * DN_HEAD_DIM
DN_CONV = 4
SWA_HEADS = 8
SWA_HEAD_DIM = 64
SWA_WIDTH = SWA_HEADS * SWA_HEAD_DIM
DILATIONS = (1, 4, 16)
ATTN_BLOCK = 128
ATTN_SPAN = 128
ROPE_THETA = 500000.0
ROPE_DIM = SWA_HEAD_DIM // 4
ROPE_HALF = ROPE_DIM // 2
FFN_CONV = 3
NORM_EPS = 1e-6

LANES = 128
SUBLANES = 8
BF16_ROWS = 16
VMEM_LIMIT_BYTES = 56 * 1024 * 1024

TOKEN_TILE = 512
DELTA_BLOCK = 256
DELTA_CHUNK = 128
INV_BASE = 8
PROJ_TILE = 512
N_PROJ_TILES = 7
ATTN_TILE0 = 4
GATE_LANES = 2 * LANES


def _params(n_axes):
    return pltpu.CompilerParams(
        dimension_semantics=("arbitrary",) * n_axes, vmem_limit_bytes=VMEM_LIMIT_BYTES)


def _sigmoid(x):
    return 1.0 / (1.0 + jnp.exp(-x))


def _rms_scale(x):
    return lax.rsqrt(jnp.mean(x * x, axis=-1, keepdims=True) + NORM_EPS)


def _split3(x):
    hi = x.astype(BF16)
    r = x - hi.astype(F32)
    mid = r.astype(BF16)
    lo = (r - mid.astype(F32)).astype(BF16)
    return hi, mid, lo


def _in_proj_kernel(x_ref, nw_ref, w_ref, wg_ref, cos_ref, sa_ref, sb_ref, proj_ref, gate_ref,
                    *rest, tm):
    res_refs, att_scr = rest[:-1], rest[-1]
    groups_per_tile = PROJ_TILE // LANES
    x = x_ref[...]
    h = (x * _rms_scale(x) * nw_ref[...]).astype(BF16)
    gate_ref[...] = jnp.dot(h, wg_ref[...], preferred_element_type=F32)
    for j in range(N_PROJ_TILES):
        cols = slice(j * PROJ_TILE, (j + 1) * PROJ_TILE)
        y = jnp.dot(h, w_ref[:, cols], preferred_element_type=F32)
        if j in (4, 5):
            if j == 4:
                y = y * (SWA_HEAD_DIM ** -0.5)
            y = (y * cos_ref[...]
                 + pltpu.roll(y, PROJ_TILE - ROPE_HALF, 1) * sa_ref[...]
                 + pltpu.roll(y, ROPE_HALF, 1) * sb_ref[...])
        proj_ref[:, cols] = y.astype(BF16)
        if j >= ATTN_TILE0:
            for g in range(groups_per_tile):
                att_scr[(j - ATTN_TILE0) * groups_per_tile + g] = y[:, g * LANES:(g + 1) * LANES]
    for dil, res_ref in zip(DILATIONS[1:], res_refs):
        for r in range(dil):
            for g in range(att_scr.shape[0]):
                res_ref[r, :, g * LANES:(g + 1) * LANES] = (
                    att_scr[g, pl.ds(r, tm // dil, stride=dil), :].astype(BF16))


def _in_proj(x, norm_w, w_main, w_gate, cos_f, sin_a, sin_b):
    B, S, D = x.shape
    tm = min(TOKEN_TILE, S)
    n_cols = N_PROJ_TILES * PROJ_TILE
    att_cols = 3 * SWA_WIDTH
    res_specs = [pl.BlockSpec((None, d, tm // d, att_cols), lambda s, b: (b, 0, s, 0))
                 for d in DILATIONS[1:]]
    res_shapes = [jax.ShapeDtypeStruct((B, d, S // d, att_cols), BF16) for d in DILATIONS[1:]]
    return pl.pallas_call(
        functools.partial(_in_proj_kernel, tm=tm),
        grid=(S // tm, B),
        in_specs=[
            pl.BlockSpec((None, tm, D), lambda s, b: (b, s, 0)),
            pl.BlockSpec((1, D), lambda s, b: (0, 0)),
            pl.BlockSpec((D, n_cols), lambda s, b: (0, 0)),
            pl.BlockSpec((D, GATE_LANES), lambda s, b: (0, 0)),
            pl.BlockSpec((tm, PROJ_TILE), lambda s, b: (s, 0)),
            pl.BlockSpec((tm, PROJ_TILE), lambda s, b: (s, 0)),
            pl.BlockSpec((tm, PROJ_TILE), lambda s, b: (s, 0)),
        ],
        out_specs=[
            pl.BlockSpec((None, tm, n_cols), lambda s, b: (b, s, 0)),
            pl.BlockSpec((None, tm, GATE_LANES), lambda s, b: (b, s, 0)),
        ] + res_specs,
        out_shape=[
            jax.ShapeDtypeStruct((B, S, n_cols), BF16),
            jax.ShapeDtypeStruct((B, S, GATE_LANES), F32),
        ] + res_shapes,
        scratch_shapes=[pltpu.VMEM((att_cols // LANES, tm, LANES), F32)],
        compiler_params=_params(2),
        name="in_proj",
    )(x, norm_w, w_main, w_gate, cos_f, sin_a, sin_b)


def _delta_kernel(qkv_ref, z_ref, gate_ref, cw_ref, alog_ref, dtb_ref, onw_ref, o_ref,
                  xs_scr, state_scr, *, tb, chunk):
    halo = SUBLANES

    @pl.when(pl.program_id(1) == 0)
    def _():
        xs_scr[0:halo, :] = jnp.zeros((halo, 3 * DN_WIDTH), F32)
        state_scr[...] = jnp.zeros_like(state_scr)

    @pl.when(pl.program_id(1) > 0)
    def _():
        xs_scr[0:halo, :] = xs_scr[tb:tb + halo, :]

    xs_scr[halo:halo + tb, :] = qkv_ref[...].astype(F32)

    def conv_silu(col0):
        cols = slice(col0, col0 + DN_HEAD_DIM)
        acc = xs_scr[halo:halo + tb, cols] * cw_ref[DN_CONV - 1:DN_CONV, cols]
        for j in range(DN_CONV - 1):
            off = halo - (DN_CONV - 1) + j
            acc = acc + xs_scr[off:off + tb, cols] * cw_ref[j:j + 1, cols]
        return acc * _sigmoid(acc)

    def l2norm(t):
        return t * lax.rsqrt(jnp.sum(t * t, axis=-1, keepdims=True) + NORM_EPS)

    gates = gate_ref[...]
    beta_all = _sigmoid(gates[:, :LANES])
    a_in = gates[:, LANES:] + dtb_ref[...]
    softplus = jnp.maximum(a_in, 0.0) + jnp.log(1.0 + jnp.exp(-jnp.abs(a_in)))
    g_all = -jnp.exp(alog_ref[...]) * softplus
    g_parts = _split3(g_all)

    r_i = lax.broadcasted_iota(jnp.int32, (chunk, chunk), 0)
    c_i = lax.broadcasted_iota(jnp.int32, (chunk, chunk), 1)
    incl = r_i >= c_i
    strict = r_i > c_i
    tri = jnp.where(incl, 1.0, 0.0).astype(BF16)
    eye = jnp.where(r_i == c_i, 1.0, 0.0).astype(F32)

    def same_block(log_size):
        return (r_i >> log_size) == (c_i >> log_size)

    log_s = int(math.log2(INV_BASE))
    level_masks = [same_block(log_s)]
    while (1 << log_s) < chunk:
        level_masks.append(same_block(log_s + 1) & (((r_i >> log_s) & 1) == 1)
                           & (((c_i >> log_s) & 1) == 0))
        log_s += 1

    qs, ks, vs = [], [], []
    for h in range(DN_HEADS):
        qs.append(l2norm(conv_silu(h * DN_HEAD_DIM)) * (DN_HEAD_DIM ** -0.5))
        ks.append(l2norm(conv_silu(DN_WIDTH + h * DN_HEAD_DIM)))
        vs.append(conv_silu(2 * DN_WIDTH + h * DN_HEAD_DIM))

    def mm(a, b):
        return jnp.dot(a.astype(BF16), b.astype(BF16), preferred_element_type=F32)

    n_chunks = tb // chunk
    probs = [(c, h) for c in range(n_chunks) for h in range(DN_HEADS)]
    row_of = lambda c: slice(c * chunk, (c + 1) * chunk)

    Gs = [sum(jnp.dot(tri, p[row_of(c)], preferred_element_type=F32) for p in g_parts)
          for c in range(n_chunks)]
    G_ts = [G.T for G in Gs]

    q_l = [qs[h][row_of(c)] for c, h in probs]
    k_l = [ks[h][row_of(c)] for c, h in probs]
    v_l = [vs[h][row_of(c)] for c, h in probs]
    gcol_l = [Gs[c][:, h:h + 1] for c, h in probs]
    grow_l = [G_ts[c][h:h + 1, :] for c, h in probs]
    glast_l = [g[:, chunk - 1:chunk] for g in grow_l]
    bcol_l = [beta_all[row_of(c), h:h + 1] for c, h in probs]
    gamma_l = [jnp.exp(jnp.where(incl, gc - gr, -jnp.inf)) for gc, gr in zip(gcol_l, grow_l)]
    eg_l = [jnp.exp(g) for g in gcol_l]
    kb_l = [k * b for k, b in zip(k_l, bcol_l)]
    kkqk_l = [lax.dot_general(jnp.concatenate([kb, q], axis=0).astype(BF16), k.astype(BF16),
                              (((1,), (1,)), ((), ())), preferred_element_type=F32)
              for kb, q, k in zip(kb_l, q_l, k_l)]
    a_l = [jnp.where(strict, kk[:chunk] * gm, 0.0) for kk, gm in zip(kkqk_l, gamma_l)]
    qk_l = [kk[chunk:] * gm for kk, gm in zip(kkqk_l, gamma_l)]

    p_l = [jnp.where(level_masks[0], -a, 0.0) for a in a_l]
    t_l = [eye + x for x in p_l]
    for _ in range(int(math.log2(INV_BASE)) - 1):
        p_l = [mm(p, p) for p in p_l]
        t_l = [t + mm(p, t) for p, t in zip(p_l, t_l)]
    for lvl_mask in level_masks[1:]:
        tb_l = [t.astype(BF16) for t in t_l]
        ld_l = [jnp.dot(jnp.where(lvl_mask, a, 0.0).astype(BF16), t_b, preferred_element_type=F32)
                for a, t_b in zip(a_l, tb_l)]
        t_l = [t - jnp.dot(t_b, ld.astype(BF16), preferred_element_type=F32)
               for t, t_b, ld in zip(t_l, tb_l, ld_l)]
    sol_l = [mm(t, jnp.concatenate([v * b, kb * eg], axis=1))
             for t, v, b, kb, eg in zip(t_l, v_l, bcol_l, kb_l, eg_l)]
    wq_l = [jnp.concatenate([sol[:, DN_HEAD_DIM:], q * eg], axis=0).astype(BF16)
            for sol, q, eg in zip(sol_l, q_l, eg_l)]
    kdt_l = [(k * jnp.exp(gl - gc)).T.astype(BF16) for k, gl, gc in zip(k_l, glast_l, gcol_l)]

    states = [state_scr[h] for h in range(DN_HEADS)]
    for c in range(n_chunks):
        ids = [c * DN_HEADS + h for h in range(DN_HEADS)]
        ws_l = [jnp.dot(wq_l[i], states[h].astype(BF16), preferred_element_type=F32)
                for h, i in enumerate(ids)]
        vnew_l = [(sol_l[i][:, :DN_HEAD_DIM] - ws[:chunk]).astype(BF16) for i, ws in zip(ids, ws_l)]
        o_l = [ws[chunk:] + jnp.dot(qk_l[i].astype(BF16), vn, preferred_element_type=F32)
               for i, ws, vn in zip(ids, ws_l, vnew_l)]
        states = [states[h] * jnp.exp(glast_l[i]) + jnp.dot(kdt_l[i], vn, preferred_element_type=F32)
                  for (h, i), vn in zip(enumerate(ids), vnew_l)]
        for h, o in enumerate(o_l):
            cols = slice(h * DN_HEAD_DIM, (h + 1) * DN_HEAD_DIM)
            zf = z_ref[row_of(c), cols].astype(F32)
            o = o * _rms_scale(o) * onw_ref[...] * (zf * _sigmoid(zf))
            o_ref[row_of(c), cols] = o.astype(BF16)
    for h in range(DN_HEADS):
        state_scr[h] = states[h]


def _delta_mixer(proj, gates, conv_w, a_log, dt_bias, out_norm_w):
    B, S, _ = proj.shape
    tb = min(DELTA_BLOCK, S)
    chunk = min(DELTA_CHUNK, tb)
    kern = functools.partial(_delta_kernel, tb=tb, chunk=chunk)
    return pl.pallas_call(
        kern,
        grid=(B, S // tb),
        in_specs=[
            pl.BlockSpec((None, tb, 3 * DN_WIDTH), lambda b, s: (b, s, 0)),
            pl.BlockSpec((None, tb, DN_WIDTH), lambda b, s: (b, s, 3)),
            pl.BlockSpec((None, tb, GATE_LANES), lambda b, s: (b, s, 0)),
            pl.BlockSpec((DN_CONV, 3 * DN_WIDTH), lambda b, s: (0, 0)),
            pl.BlockSpec((1, LANES), lambda b, s: (0, 0)),
            pl.BlockSpec((1, LANES), lambda b, s: (0, 0)),
            pl.BlockSpec((1, DN_HEAD_DIM), lambda b, s: (0, 0)),
        ],
        out_specs=pl.BlockSpec((None, tb, DN_WIDTH), lambda b, s: (b, s, 0)),
        out_shape=jax.ShapeDtypeStruct((B, S, DN_WIDTH), BF16),
        scratch_shapes=[
            pltpu.VMEM((tb + SUBLANES, 3 * DN_WIDTH), F32),
            pltpu.VMEM((DN_HEADS, DN_HEAD_DIM, DN_HEAD_DIM), F32),
        ],
        compiler_params=_params(2),
        name="delta_mixer",
    )(proj, proj, gates, conv_w, a_log, dt_bias, out_norm_w)


def _attn_kernel(q_ref, kp_ref, kc_ref, vp_ref, vc_ref, o_ref, lse_ref, *, bq):
    blk = ATTN_BLOCK
    q_i = lax.broadcasted_iota(jnp.int32, (blk, 2 * blk), 0)
    k_i = lax.broadcasted_iota(jnp.int32, (blk, 2 * blk), 1)
    band = (k_i >= q_i) & (k_i <= q_i + ATTN_SPAN)
    first_lo = jnp.where(pl.program_id(2) == 0, blk, 0)
    band_first = band & (k_i >= first_lo)
    lane = lax.broadcasted_iota(jnp.int32, (blk, LANES), 1)
    low_half = lane < SWA_HEAD_DIM

    for s in range(bq // blk):
        rows = slice(s * blk, (s + 1) * blk)
        q = q_ref[rows, :]
        if s == 0:
            k = jnp.concatenate([kp_ref[...], kc_ref[0:blk, :]], axis=0)
            v = jnp.concatenate([vp_ref[...], vc_ref[0:blk, :]], axis=0)
            valid = band_first
        else:
            k = kc_ref[(s - 1) * blk:(s + 1) * blk, :]
            v = vc_ref[(s - 1) * blk:(s + 1) * blk, :]
            valid = band
        lse_acc = jnp.zeros((blk, LANES), F32)
        for pair in range(SWA_HEADS // 2):
            cols = slice(pair * LANES, (pair + 1) * LANES)
            q2, k2, v2 = q[:, cols], k[:, cols], v[:, cols]
            halves = []
            for half in range(2):
                mask = low_half if half == 0 else jnp.logical_not(low_half)
                qh = jnp.where(mask, q2, jnp.zeros_like(q2))
                sc = lax.dot_general(qh, k2, (((1,), (1,)), ((), ())), preferred_element_type=F32)
                sc = jnp.where(valid, sc, -jnp.inf)
                m = jnp.max(sc, axis=-1, keepdims=True)
                p = jnp.exp(sc - m)
                l = jnp.sum(p, axis=-1, keepdims=True)
                halves.append(jnp.dot(p.astype(BF16), v2, preferred_element_type=F32) / l)
                lse_acc = jnp.where(lane == 2 * pair + half, m + jnp.log(l), lse_acc)
            o_ref[rows, cols] = jnp.where(low_half, halves[0], halves[1]).astype(BF16)
        lse_ref[rows, :] = lse_acc


def _dilated_attention_pass(src, dil):
    blk = ATTN_BLOCK
    if dil == 1:
        B, L, _ = src.shape
        tile0 = ATTN_TILE0
        src = src.reshape(B, 1, L, src.shape[-1])
    else:
        B, _, L, _ = src.shape
        tile0 = 0
    bq = min(512, L)
    sub = bq // blk
    kern = functools.partial(_attn_kernel, bq=bq)

    def cur(col):
        return pl.BlockSpec((None, None, bq, PROJ_TILE), lambda b, r, i: (b, r, i, tile0 + col))

    def prev(col):
        return pl.BlockSpec((None, None, blk, PROJ_TILE),
                            lambda b, r, i: (b, r, jnp.maximum(i * sub - 1, 0), tile0 + col))

    return pl.pallas_call(
        kern,
        grid=(B, dil, L // bq),
        in_specs=[cur(0), prev(1), cur(1), prev(2), cur(2)],
        out_specs=[
            pl.BlockSpec((None, None, bq, SWA_WIDTH), lambda b, r, i: (b, r, i, 0)),
            pl.BlockSpec((None, None, bq, LANES), lambda b, r, i: (b, r, i, 0)),
        ],
        out_shape=[
            jax.ShapeDtypeStruct((B, dil, L, SWA_WIDTH), BF16),
            jax.ShapeDtypeStruct((B, dil, L, LANES), F32),
        ],
        compiler_params=_params(3),
        name=f"dilated_attn_d{dil}",
    )(src, src, src, src, src)


def _out_proj_kernel(oa_ref, o1_ref, o2_ref, o3_ref, l1_ref, l2_ref, l3_ref, x_ref, w_ref,
                     e_ref, nw_ref, out_ref, o_scr, l_scr, *, tm):
    def token_order(o_ref, l_ref, dil, slot):
        if dil == 1:
            return o_ref[0].astype(F32), l_ref[0]
        n_groups = SWA_WIDTH // LANES
        for r in range(dil):
            rows = pl.ds(r, tm // dil, stride=dil)
            o_res = o_ref[r].astype(F32)
            for g in range(n_groups):
                o_scr[slot * n_groups + g, rows, :] = o_res[:, g * LANES:(g + 1) * LANES]
            l_scr[slot, rows, :] = l_ref[r]
        o_tok = jnp.concatenate([o_scr[slot * n_groups + g] for g in range(n_groups)], axis=1)
        return o_tok, l_scr[slot]

    pairs = [token_order(o, l, d, i) for i, (o, l, d) in enumerate(
        zip((o1_ref, o2_ref, o3_ref), (l1_ref, l2_ref, l3_ref), DILATIONS))]
    lses = [l for _, l in pairs]
    m = jnp.maximum(jnp.maximum(lses[0], lses[1]), lses[2])
    es = [jnp.exp(l - m) for l in lses]
    den = es[0] + es[1] + es[2]

    def expand(wt):
        hi = wt.astype(BF16)
        lo = (wt - hi.astype(F32)).astype(BF16)
        return (jnp.dot(hi, e_ref[...], preferred_element_type=F32)
                + jnp.dot(lo, e_ref[...], preferred_element_type=F32))

    ob = None
    for e, (o, _) in zip(es, pairs):
        term = expand(e / den) * o
        ob = term if ob is None else ob + term
    y = (jnp.dot(oa_ref[...], w_ref[0:DN_WIDTH, :], preferred_element_type=F32)
         + jnp.dot(ob.astype(BF16), w_ref[DN_WIDTH:, :], preferred_element_type=F32))
    out_ref[...] = x_ref[...] + y * _rms_scale(y) * nw_ref[...]


def _out_proj(o_a, outs, lses, x, w_out, expand_mat, norm_w):
    B, S, D = x.shape
    tm = min(TOKEN_TILE, S)

    def tok(width):
        return pl.BlockSpec((None, tm, width), lambda b, s: (b, s, 0))

    def grouped(width):
        return [pl.BlockSpec((None, d, tm // d, width), lambda b, s: (b, 0, s, 0)) for d in DILATIONS]

    n_slots = len(DILATIONS)
    return pl.pallas_call(
        functools.partial(_out_proj_kernel, tm=tm),
        grid=(B, S // tm),
        in_specs=[tok(DN_WIDTH)] + grouped(SWA_WIDTH) + grouped(LANES) + [
            tok(D),
            pl.BlockSpec((DN_WIDTH + SWA_WIDTH, D), lambda b, s: (0, 0)),
            pl.BlockSpec((LANES, SWA_WIDTH), lambda b, s: (0, 0)),
            pl.BlockSpec((1, D), lambda b, s: (0, 0))],
        out_specs=tok(D),
        out_shape=jax.ShapeDtypeStruct((B, S, D), F32),
        scratch_shapes=[pltpu.VMEM((n_slots * SWA_WIDTH // LANES, tm, LANES), F32),
                        pltpu.VMEM((n_slots, tm, LANES), F32)],
        compiler_params=_params(2),
        name="out_proj",
    )(o_a, *outs, *lses, x, w_out, expand_mat, norm_w)


def _ffn_kernel(x_ref, xh_ref, nw_ref, wg_ref, wv_ref, cg_ref, cv_ref, bg_ref, bv_ref, wd_ref,
                pnw_ref, out_ref, h_scr, u_scr, acc_scr, *, tm, tiles_per_seq):
    halo = BF16_ROWS
    j = pl.program_id(1)

    @pl.when(j == 0)
    def _():
        x = x_ref[...]
        h_scr[halo:, :] = (x * _rms_scale(x) * nw_ref[...]).astype(BF16)
        xh = xh_ref[...]
        hh = xh * _rms_scale(xh) * nw_ref[...]
        starts_seq = (pl.program_id(0) % tiles_per_seq) == 0
        h_scr[0:halo, :] = jnp.where(starts_seq, 0.0, hh).astype(BF16)
        acc_scr[...] = jnp.zeros_like(acc_scr)

    def conv(w_ref, c_ref, b_ref):
        u_scr[...] = jnp.dot(h_scr[...], w_ref[...], preferred_element_type=F32)
        acc = u_scr[halo:halo + tm, :] * c_ref[FFN_CONV - 1:FFN_CONV, :]
        for t in range(FFN_CONV - 1):
            off = halo - (FFN_CONV - 1) + t
            acc = acc + u_scr[off:off + tm, :] * c_ref[t:t + 1, :]
        return acc + b_ref[...]

    gate = conv(wg_ref, cg_ref, bg_ref)
    cdf = 0.5 * (1.0 + jnp.tanh(math.sqrt(2.0 / math.pi) * (gate + 0.044715 * (gate * gate * gate))))
    act = (gate * cdf) * conv(wv_ref, cv_ref, bv_ref)
    acc_scr[...] += jnp.dot(act.astype(BF16), wd_ref[...], preferred_element_type=F32)

    @pl.when(j == pl.num_programs(1) - 1)
    def _():
        y = acc_scr[...]
        out_ref[...] = x_ref[...] + y * _rms_scale(y) * pnw_ref[...]


def _ffn(x, pre_w, w_up, conv_w, conv_b, w_down, post_w):
    B, S, D = x.shape
    T = B * S
    d_ff = w_down.shape[0]
    n_chunks = 2
    fc = d_ff // n_chunks
    tm = min(TOKEN_TILE, S)
    halo = BF16_ROWS
    hb = tm // halo
    xf = x.reshape(T, D)
    kern = functools.partial(_ffn_kernel, tm=tm, tiles_per_seq=S // tm)
    out = pl.pallas_call(
        kern,
        grid=(T // tm, n_chunks),
        in_specs=[
            pl.BlockSpec((tm, D), lambda i, j: (i, 0)),
            pl.BlockSpec((halo, D), lambda i, j: (jnp.maximum(i * hb - 1, 0), 0)),
            pl.BlockSpec((1, D), lambda i, j: (0, 0)),
            pl.BlockSpec((D, fc), lambda i, j: (0, j)),
            pl.BlockSpec((D, fc), lambda i, j: (0, n_chunks + j)),
            pl.BlockSpec((FFN_CONV, fc), lambda i, j: (0, j)),
            pl.BlockSpec((FFN_CONV, fc), lambda i, j: (0, n_chunks + j)),
            pl.BlockSpec((1, fc), lambda i, j: (0, j)),
            pl.BlockSpec((1, fc), lambda i, j: (0, n_chunks + j)),
            pl.BlockSpec((fc, D), lambda i, j: (j, 0)),
            pl.BlockSpec((1, D), lambda i, j: (0, 0)),
        ],
        out_specs=pl.BlockSpec((tm, D), lambda i, j: (i, 0)),
        out_shape=jax.ShapeDtypeStruct((T, D), F32),
        scratch_shapes=[
            pltpu.VMEM((tm + halo, D), BF16),
            pltpu.VMEM((tm + halo, fc), F32),
            pltpu.VMEM((tm, D), F32),
        ],
        compiler_params=_params(2),
        name="conv_glu_ffn",
    )(xf, xf, pre_w, w_up, w_up, conv_w, conv_w, conv_b, conv_b, w_down, post_w)
    return out.reshape(B, S, D)


def _rope_tables(S):
    pos = jnp.arange(S, dtype=F32)
    inv_freq = ROPE_THETA ** (-jnp.arange(0, ROPE_DIM, 2, dtype=F32) / ROPE_DIM)
    ang = pos[:, None] * inv_freq[None, :]
    cos, sin = jnp.cos(ang), jnp.sin(ang)
    pad = SWA_HEAD_DIM - ROPE_DIM
    ones = jnp.ones((S, pad), F32)
    zeros_h = jnp.zeros((S, ROPE_HALF), F32)
    zeros_p = jnp.zeros((S, pad), F32)
    cos_head = jnp.concatenate([cos, cos, ones], axis=1)
    sa_head = jnp.concatenate([-sin, zeros_h, zeros_p], axis=1)
    sb_head = jnp.concatenate([zeros_h, sin, zeros_p], axis=1)
    tile = lambda t: jnp.tile(t, (1, SWA_HEADS))
    return tile(cos_head), tile(sa_head), tile(sb_head)


def _pad_lanes(v):
    return jnp.zeros((1, LANES), F32).at[0, :v.shape[0]].set(v.astype(F32))


def kernel(x, pre_mix_norm, w_in, dn_conv, dn_a_log, dn_dt_bias, dn_out_norm, w_out, post_mix_norm,
           pre_ffn_norm, ffn_up, ffn_conv, ffn_conv_bias, ffn_down, post_ffn_norm):
    B, S, D = x.shape
    depth = w_in.shape[0]
    assert S % (max(DILATIONS) * ATTN_BLOCK) == 0 and S % TOKEN_TILE == 0
    cos_f, sin_a, sin_b = _rope_tables(S)
    qkvz = 4 * DN_WIDTH
    n_gate = 2 * DN_HEADS
    head_of_lane = jnp.arange(SWA_WIDTH) // SWA_HEAD_DIM
    expand_mat = (jnp.arange(LANES)[:, None] == head_of_lane[None, :]).astype(BF16)
    row = lambda v: v.reshape(1, -1).astype(F32)

    for l in range(depth):
        w_l = w_in[l]
        w_main = jnp.concatenate([w_l[:, :qkvz], w_l[:, qkvz + n_gate:]], axis=1).astype(BF16)
        w_gate = jnp.zeros((D, GATE_LANES), F32)
        w_gate = w_gate.at[:, :DN_HEADS].set(w_l[:, qkvz:qkvz + DN_HEADS])
        w_gate = w_gate.at[:, LANES:LANES + DN_HEADS].set(w_l[:, qkvz + DN_HEADS:qkvz + n_gate])
        proj, gates, *grouped = _in_proj(x, row(pre_mix_norm[l]), w_main, w_gate.astype(BF16),
                                         cos_f, sin_a, sin_b)
        o_a = _delta_mixer(proj, gates, dn_conv[l].astype(F32), _pad_lanes(dn_a_log[l]),
                           _pad_lanes(dn_dt_bias[l]), row(dn_out_norm[l]))
        outs, lses = zip(*[_dilated_attention_pass(src, d)
                           for src, d in zip([proj] + grouped, DILATIONS)])
        x = _out_proj(o_a, outs, lses, x, w_out[l].astype(BF16), expand_mat, row(post_mix_norm[l]))
        x = _ffn(x, row(pre_ffn_norm[l]), ffn_up[l].astype(BF16), ffn_conv[l].astype(F32),
                 row(ffn_conv_bias[l]), ffn_down[l].astype(BF16), row(post_ffn_norm[l]))
    return x
```

```python
import functools
import math

import jax
import jax.numpy as jnp
from jax import lax
from jax.experimental import pallas as pl
from jax.experimental.pallas import tpu as pltpu

F32 = jnp.float32
BF16 = jnp.bfloat16

DN_HEADS = 4
DN_HEAD_DIM = 128
DN_WIDTH = DN_HEADS * DN_HEAD_DIM
DN_CONV = 4
SWA_HEADS = 8
SWA_HEAD_DIM = 64
SWA_WIDTH = SWA_HEADS * SWA_HEAD_DIM
DILATIONS = (1, 4, 16)
DIL_STEP = 4
ATTN_BLOCK = 128
ATTN_SPAN = 128
ROPE_THETA = 500000.0
ROPE_DIM = SWA_HEAD_DIM // 4
ROPE_HALF = ROPE_DIM // 2
FFN_CONV = 3
NORM_EPS = 1e-6

LANES = 128
SUBLANES = 8
BF16_ROWS = 16
VMEM_LIMIT_BYTES = 56 * 1024 * 1024

TOKEN_TILE = 512
DELTA_BLOCK = 512
DELTA_CHUNK = 128
INV_BASE = 8
PROJ_TILE = 512
N_PROJ_TILES = 7
ATTN_TILE0 = 4
GATE_LANES = 2 * LANES


def _params(n_axes):
    return pltpu.CompilerParams(
        dimension_semantics=("arbitrary",) * n_axes, vmem_limit_bytes=VMEM_LIMIT_BYTES)


def _sigmoid(x):
    return 1.0 / (1.0 + jnp.exp(-x))


def _rms_scale(x):
    return lax.rsqrt(jnp.mean(x * x, axis=-1, keepdims=True) + NORM_EPS)


def _split3(x):
    hi = x.astype(BF16)
    r = x - hi.astype(F32)
    mid = r.astype(BF16)
    lo = (r - mid.astype(F32)).astype(BF16)
    return hi, mid, lo


def _in_proj_kernel(x_ref, xh_ref, nw_ref, w_ref, wg_ref, cw_ref, cos_ref, sa_ref, sb_ref,
                    proj_ref, gate_ref, *rest, tm):
    res_refs, conv_scr, att_scr, mid_scr = rest[:-3], rest[-3], rest[-2], rest[-1]
    groups_per_tile = PROJ_TILE // LANES
    halo = BF16_ROWS
    x = x_ref[...]
    h = (x * _rms_scale(x) * nw_ref[...]).astype(BF16)
    xh = xh_ref[...]
    hh = jnp.where(pl.program_id(0) == 0, 0.0, xh * _rms_scale(xh) * nw_ref[...]).astype(BF16)
    h_ext = jnp.concatenate([hh, h], axis=0)
    gate_ref[...] = jnp.dot(h, wg_ref[...], preferred_element_type=F32)

    def conv_silu_norm(j, cols):
        conv_scr[j] = jnp.dot(h_ext, w_ref[:, cols], preferred_element_type=F32)
        acc = conv_scr[j, halo:halo + tm, :] * cw_ref[DN_CONV - 1:DN_CONV, cols]
        for t in range(DN_CONV - 1):
            off = halo - (DN_CONV - 1) + t
            acc = acc + conv_scr[j, off:off + tm, :] * cw_ref[t:t + 1, cols]
        y = acc * _sigmoid(acc)
        if j == 2:
            return y
        scale = DN_HEAD_DIM ** -0.5 if j == 0 else 1.0
        heads = []
        for hd in range(DN_HEADS):
            t = y[:, hd * DN_HEAD_DIM:(hd + 1) * DN_HEAD_DIM]
            heads.append(t * (lax.rsqrt(jnp.sum(t * t, axis=-1, keepdims=True) + NORM_EPS) * scale))
        return jnp.concatenate(heads, axis=1)

    def regroup(planes):
        src_scr, rows_per_group = att_scr, tm
        for level, res_ref in enumerate(res_refs):
            n_out = rows_per_group // DIL_STEP
            for grp in range(DIL_STEP ** level):
                for step in range(DIL_STEP):
                    r = step * DIL_STEP ** level + grp
                    start = grp * rows_per_group + step
                    for g in planes:
                        t = src_scr[g, pl.ds(start, n_out, stride=DIL_STEP), :]
                        res_ref[r, :, g * LANES:(g + 1) * LANES] = t.astype(BF16)
                        if level + 1 < len(res_refs):
                            mid_scr[g, r * n_out:(r + 1) * n_out, :] = t
            src_scr, rows_per_group = mid_scr, n_out

    tile_order = list(range(ATTN_TILE0, N_PROJ_TILES)) + list(range(ATTN_TILE0))
    for j in tile_order:
        if j < N_PROJ_TILES - ATTN_TILE0:
            regroup(range(j * groups_per_tile, (j + 1) * groups_per_tile))
        cols = slice(j * PROJ_TILE, (j + 1) * PROJ_TILE)
        if j < 3:
            proj_ref[:, cols] = conv_silu_norm(j, cols).astype(BF16)
            continue
        y = jnp.dot(h, w_ref[:, cols], preferred_element_type=F32)
        if j in (4, 5):
            if j == 4:
                y = y * (SWA_HEAD_DIM ** -0.5)
            y = (y * cos_ref[...]
                 + pltpu.roll(y, PROJ_TILE - ROPE_HALF, 1) * sa_ref[...]
                 + pltpu.roll(y, ROPE_HALF, 1) * sb_ref[...])
        proj_ref[:, cols] = y.astype(BF16)
        if j >= ATTN_TILE0:
            for g in range(groups_per_tile):
                att_scr[(j - ATTN_TILE0) * groups_per_tile + g] = y[:, g * LANES:(g + 1) * LANES]


def _in_proj(x, norm_w, w_main, w_gate, conv_w, cos_f, sin_a, sin_b):
    B, S, D = x.shape
    tm = min(TOKEN_TILE, S)
    halo = BF16_ROWS
    hb = tm // halo
    n_cols = N_PROJ_TILES * PROJ_TILE
    att_cols = 3 * SWA_WIDTH
    res_specs = [pl.BlockSpec((None, d, tm // d, att_cols), lambda s, b: (b, 0, s, 0))
                 for d in DILATIONS[1:]]
    res_shapes = [jax.ShapeDtypeStruct((B, d, S // d, att_cols), BF16) for d in DILATIONS[1:]]
    return pl.pallas_call(
        functools.partial(_in_proj_kernel, tm=tm),
        grid=(S // tm, B),
        in_specs=[
            pl.BlockSpec((None, tm, D), lambda s, b: (b, s, 0)),
            pl.BlockSpec((None, halo, D), lambda s, b: (b, jnp.maximum(s * hb - 1, 0), 0)),
            pl.BlockSpec((1, D), lambda s, b: (0, 0)),
            pl.BlockSpec((D, n_cols), lambda s, b: (0, 0)),
            pl.BlockSpec((D, GATE_LANES), lambda s, b: (0, 0)),
            pl.BlockSpec((DN_CONV, 3 * DN_WIDTH), lambda s, b: (0, 0)),
            pl.BlockSpec((tm, PROJ_TILE), lambda s, b: (s, 0)),
            pl.BlockSpec((tm, PROJ_TILE), lambda s, b: (s, 0)),
            pl.BlockSpec((tm, PROJ_TILE), lambda s, b: (s, 0)),
        ],
        out_specs=[
            pl.BlockSpec((None, tm, n_cols), lambda s, b: (b, s, 0)),
            pl.BlockSpec((None, tm, GATE_LANES), lambda s, b: (b, s, 0)),
        ] + res_specs,
        out_shape=[
            jax.ShapeDtypeStruct((B, S, n_cols), BF16),
            jax.ShapeDtypeStruct((B, S, GATE_LANES), F32),
        ] + res_shapes,
        scratch_shapes=[pltpu.VMEM((3, tm + halo, PROJ_TILE), F32)]
        + [pltpu.VMEM((att_cols // LANES, tm, LANES), F32)] * 2,
        compiler_params=_params(2),
        name="in_proj",
    )(x, x, norm_w, w_main, w_gate, conv_w, cos_f, sin_a, sin_b)


def _delta_kernel(qkv_ref, z_ref, gate_ref, alog_ref, dtb_ref, onw_ref, o_ref, state_scr,
                  *, tb, chunk):
    @pl.when(pl.program_id(1) == 0)
    def _():
        state_scr[...] = jnp.zeros_like(state_scr)

    gates = gate_ref[...]
    beta_all = _sigmoid(gates[:, :LANES])
    a_in = gates[:, LANES:] + dtb_ref[...]
    softplus = jnp.maximum(a_in, 0.0) + jnp.log(1.0 + jnp.exp(-jnp.abs(a_in)))
    g_all = -jnp.exp(alog_ref[...]) * softplus
    g_parts = _split3(g_all)

    r_i = lax.broadcasted_iota(jnp.int32, (chunk, chunk), 0)
    c_i = lax.broadcasted_iota(jnp.int32, (chunk, chunk), 1)
    incl = r_i >= c_i
    strict = r_i > c_i
    tri = jnp.where(incl, 1.0, 0.0).astype(BF16)
    eye = jnp.where(r_i == c_i, 1.0, 0.0).astype(F32)

    def same_block(log_size):
        return (r_i >> log_size) == (c_i >> log_size)

    log_s = int(math.log2(INV_BASE))
    level_masks = [same_block(log_s)]
    while (1 << log_s) < chunk:
        level_masks.append(same_block(log_s + 1) & (((r_i >> log_s) & 1) == 1)
                           & (((c_i >> log_s) & 1) == 0))
        log_s += 1

    def head_cols(part, h):
        return slice(part * DN_WIDTH + h * DN_HEAD_DIM, part * DN_WIDTH + (h + 1) * DN_HEAD_DIM)

    def mm(a, b):
        return jnp.dot(a.astype(BF16), b.astype(BF16), preferred_element_type=F32)

    n_chunks = tb // chunk
    probs = [(c, h) for c in range(n_chunks) for h in range(DN_HEADS)]
    row_of = lambda c: slice(c * chunk, (c + 1) * chunk)

    Gs = [sum(jnp.dot(tri, p[row_of(c)], preferred_element_type=F32) for p in g_parts)
          for c in range(n_chunks)]
    G_ts = [G.T for G in Gs]

    q_l = [qkv_ref[row_of(c), head_cols(0, h)].astype(F32) for c, h in probs]
    k_l = [qkv_ref[row_of(c), head_cols(1, h)].astype(F32) for c, h in probs]
    v_l = [qkv_ref[row_of(c), head_cols(2, h)].astype(F32) for c, h in probs]
    gcol_l = [Gs[c][:, h:h + 1] for c, h in probs]
    grow_l = [G_ts[c][h:h + 1, :] for c, h in probs]
    glast_l = [g[:, chunk - 1:chunk] for g in grow_l]
    bcol_l = [beta_all[row_of(c), h:h + 1] for c, h in probs]
    gamma_l = [jnp.exp(jnp.where(incl, gc - gr, -jnp.inf)) for gc, gr in zip(gcol_l, grow_l)]
    eg_l = [jnp.exp(g) for g in gcol_l]
    kb_l = [k * b for k, b in zip(k_l, bcol_l)]
    kkqk_l = [lax.dot_general(jnp.concatenate([kb, q], axis=0).astype(BF16), k.astype(BF16),
                              (((1,), (1,)), ((), ())), preferred_element_type=F32)
              for kb, q, k in zip(kb_l, q_l, k_l)]
    a_l = [jnp.where(strict, kk[:chunk] * gm, 0.0) for kk, gm in zip(kkqk_l, gamma_l)]
    qk_l = [kk[chunk:] * gm for kk, gm in zip(kkqk_l, gamma_l)]

    p_l = [jnp.where(level_masks[0], -a, 0.0) for a in a_l]
    t_l = [eye + x for x in p_l]
    for _ in range(int(math.log2(INV_BASE)) - 1):
        p_l = [mm(p, p) for p in p_l]
        t_l = [t + mm(p, t) for p, t in zip(p_l, t_l)]
    for lvl_mask in level_masks[1:]:
        tb_l = [t.astype(BF16) for t in t_l]
        ld_l = [jnp.dot(jnp.where(lvl_mask, a, 0.0).astype(BF16), t_b, preferred_element_type=F32)
                for a, t_b in zip(a_l, tb_l)]
        t_l = [t - jnp.dot(t_b, ld.astype(BF16), preferred_element_type=F32)
               for t, t_b, ld in zip(t_l, tb_l, ld_l)]
    sol_l = [mm(t, jnp.concatenate([v * b, kb * eg], axis=1))
             for t, v, b, kb, eg in zip(t_l, v_l, bcol_l, kb_l, eg_l)]
    wq_l = [jnp.concatenate([sol[:, DN_HEAD_DIM:], q * eg], axis=0).astype(BF16)
            for sol, q, eg in zip(sol_l, q_l, eg_l)]
    kdt_l = [(k * jnp.exp(gl - gc)).T.astype(BF16) for k, gl, gc in zip(k_l, glast_l, gcol_l)]

    states = [state_scr[h] for h in range(DN_HEADS)]
    for c in range(n_chunks):
        ids = [c * DN_HEADS + h for h in range(DN_HEADS)]
        ws_l = [jnp.dot(wq_l[i], states[h].astype(BF16), preferred_element_type=F32)
                for h, i in enumerate(ids)]
        vnew_l = [(sol_l[i][:, :DN_HEAD_DIM] - ws[:chunk]).astype(BF16) for i, ws in zip(ids, ws_l)]
        o_l = [ws[chunk:] + jnp.dot(qk_l[i].astype(BF16), vn, preferred_element_type=F32)
               for i, ws, vn in zip(ids, ws_l, vnew_l)]
        states = [states[h] * jnp.exp(glast_l[i]) + jnp.dot(kdt_l[i], vn, preferred_element_type=F32)
                  for (h, i), vn in zip(enumerate(ids), vnew_l)]
        for h, o in enumerate(o_l):
            cols = slice(h * DN_HEAD_DIM, (h + 1) * DN_HEAD_DIM)
            zf = z_ref[row_of(c), cols].astype(F32)
            o = o * _rms_scale(o) * onw_ref[...] * (zf * _sigmoid(zf))
            o_ref[row_of(c), cols] = o.astype(BF16)
    for h in range(DN_HEADS):
        state_scr[h] = states[h]


def _delta_mixer(proj, gates, a_log, dt_bias, out_norm_w):
    B, S, _ = proj.shape
    tb = min(DELTA_BLOCK, S)
    chunk = min(DELTA_CHUNK, tb)
    kern = functools.partial(_delta_kernel, tb=tb, chunk=chunk)
    return pl.pallas_call(
        kern,
        grid=(B, S // tb),
        in_specs=[
            pl.BlockSpec((None, tb, 3 * DN_WIDTH), lambda b, s: (b, s, 0)),
            pl.BlockSpec((None, tb, DN_WIDTH), lambda b, s: (b, s, 3)),
            pl.BlockSpec((None, tb, GATE_LANES), lambda b, s: (b, s, 0)),
            pl.BlockSpec((1, LANES), lambda b, s: (0, 0)),
            pl.BlockSpec((1, LANES), lambda b, s: (0, 0)),
            pl.BlockSpec((1, DN_HEAD_DIM), lambda b, s: (0, 0)),
        ],
        out_specs=pl.BlockSpec((None, tb, DN_WIDTH), lambda b, s: (b, s, 0)),
        out_shape=jax.ShapeDtypeStruct((B, S, DN_WIDTH), BF16),
        scratch_shapes=[pltpu.VMEM((DN_HEADS, DN_HEAD_DIM, DN_HEAD_DIM), F32)],
        compiler_params=_params(2),
        name="delta_mixer",
    )(proj, proj, gates, a_log, dt_bias, out_norm_w)


def _attn_kernel(q_ref, kp_ref, kc_ref, vp_ref, vc_ref, o_ref, lse_ref, *, bq):
    blk = ATTN_BLOCK
    q_i = lax.broadcasted_iota(jnp.int32, (blk, 2 * blk), 0)
    k_i = lax.broadcasted_iota(jnp.int32, (blk, 2 * blk), 1)
    band = (k_i >= q_i) & (k_i <= q_i + ATTN_SPAN)
    first_lo = jnp.where(pl.program_id(2) == 0, blk, 0)
    band_first = band & (k_i >= first_lo)
    lane = lax.broadcasted_iota(jnp.int32, (blk, LANES), 1)
    low_half = lane < SWA_HEAD_DIM

    for s in range(bq // blk):
        rows = slice(s * blk, (s + 1) * blk)
        q = q_ref[rows, :]
        if s == 0:
            k = jnp.concatenate([kp_ref[...], kc_ref[0:blk, :]], axis=0)
            v = jnp.concatenate([vp_ref[...], vc_ref[0:blk, :]], axis=0)
            valid = band_first
        else:
            k = kc_ref[(s - 1) * blk:(s + 1) * blk, :]
            v = vc_ref[(s - 1) * blk:(s + 1) * blk, :]
            valid = band
        lse_acc = jnp.zeros((blk, LANES), F32)
        for pair in range(SWA_HEADS // 2):
            cols = slice(pair * LANES, (pair + 1) * LANES)
            q2, k2, v2 = q[:, cols], k[:, cols], v[:, cols]
            halves = []
            for half in range(2):
                mask = low_half if half == 0 else jnp.logical_not(low_half)
                qh = jnp.where(mask, q2, jnp.zeros_like(q2))
                sc = lax.dot_general(qh, k2, (((1,), (1,)), ((), ())), preferred_element_type=F32)
                sc = jnp.where(valid, sc, -jnp.inf)
                m = jnp.max(sc, axis=-1, keepdims=True)
                p = jnp.exp(sc - m)
                l = jnp.sum(p, axis=-1, keepdims=True)
                halves.append(jnp.dot(p.astype(BF16), v2, preferred_element_type=F32) / l)
                lse_acc = jnp.where(lane == 2 * pair + half, m + jnp.log(l), lse_acc)
            o_ref[rows, cols] = jnp.where(low_half, halves[0], halves[1]).astype(BF16)
        lse_ref[rows, :] = lse_acc


def _dilated_attention_pass(src, dil):
    blk = ATTN_BLOCK
    if dil == 1:
        B, L, _ = src.shape
        tile0 = ATTN_TILE0
        src = src.reshape(B, 1, L, src.shape[-1])
    else:
        B, _, L, _ = src.shape
        tile0 = 0
    bq = min(512, L)
    sub = bq // blk
    kern = functools.partial(_attn_kernel, bq=bq)

    def cur(col):
        return pl.BlockSpec((None, None, bq, PROJ_TILE), lambda b, r, i: (b, r, i, tile0 + col))

    def prev(col):
        return pl.BlockSpec((None, None, blk, PROJ_TILE),
                            lambda b, r, i: (b, r, jnp.maximum(i * sub - 1, 0), tile0 + col))

    return pl.pallas_call(
        kern,
        grid=(B, dil, L // bq),
        in_specs=[cur(0), prev(1), cur(1), prev(2), cur(2)],
        out_specs=[
            pl.BlockSpec((None, None, bq, SWA_WIDTH), lambda b, r, i: (b, r, i, 0)),
            pl.BlockSpec((None, None, bq, LANES), lambda b, r, i: (b, r, i, 0)),
        ],
        out_shape=[
            jax.ShapeDtypeStruct((B, dil, L, SWA_WIDTH), BF16),
            jax.ShapeDtypeStruct((B, dil, L, LANES), F32),
        ],
        compiler_params=_params(3),
        name=f"dilated_attn_d{dil}",
    )(src, src, src, src, src)


def _out_proj_kernel(oa_ref, o1_ref, o2_ref, o3_ref, l1_ref, l2_ref, l3_ref, x_ref, w_ref,
                     e_ref, nw_ref, out_ref, o_scr, l_scr, *, tm):
    n_planes = SWA_WIDTH // LANES
    tmp_slot = len(DILATIONS) - 1

    def token_order(o_ref, l_ref, level):
        if level == 0:
            return o_ref[0].astype(F32), l_ref[0]

        def from_input(r):
            o_res = o_ref[r].astype(F32)
            return [o_res[:, g * LANES:(g + 1) * LANES] for g in range(n_planes)], l_ref[r]

        get, rows = from_input, tm // DIL_STEP ** level
        for lvl in range(level, 0, -1):
            dst = level - 1 if lvl == 1 else tmp_slot
            for grp in range(DIL_STEP ** (lvl - 1)):
                for step in range(DIL_STEP):
                    planes, l_res = get(step * DIL_STEP ** (lvl - 1) + grp)
                    dst_rows = pl.ds(grp * rows * DIL_STEP + step, rows, stride=DIL_STEP)
                    for g in range(n_planes):
                        o_scr[dst * n_planes + g, dst_rows, :] = planes[g]
                    l_scr[dst, dst_rows, :] = l_res
            rows *= DIL_STEP

            def get(r, src=dst, n=rows):
                blk = slice(r * n, (r + 1) * n)
                return [o_scr[src * n_planes + g, blk, :] for g in range(n_planes)], l_scr[src, blk, :]

        planes, l_tok = get(0)
        return jnp.concatenate(planes, axis=1), l_tok

    pairs = [token_order(o, l, lvl) for lvl, (o, l) in enumerate(
        zip((o1_ref, o2_ref, o3_ref), (l1_ref, l2_ref, l3_ref)))]
    lses = [l for _, l in pairs]
    m = jnp.maximum(jnp.maximum(lses[0], lses[1]), lses[2])
    es = [jnp.exp(l - m) for l in lses]
    den = es[0] + es[1] + es[2]

    def expand(wt):
        hi = wt.astype(BF16)
        lo = (wt - hi.astype(F32)).astype(BF16)
        return (jnp.dot(hi, e_ref[...], preferred_element_type=F32)
                + jnp.dot(lo, e_ref[...], preferred_element_type=F32))

    ob = None
    for e, (o, _) in zip(es, pairs):
        term = expand(e / den) * o
        ob = term if ob is None else ob + term
    y = (jnp.dot(oa_ref[...], w_ref[0:DN_WIDTH, :], preferred_element_type=F32)
         + jnp.dot(ob.astype(BF16), w_ref[DN_WIDTH:, :], preferred_element_type=F32))
    out_ref[...] = x_ref[...] + y * _rms_scale(y) * nw_ref[...]


def _out_proj(o_a, outs, lses, x, w_out, expand_mat, norm_w):
    B, S, D = x.shape
    tm = min(TOKEN_TILE, S)

    def tok(width):
        return pl.BlockSpec((None, tm, width), lambda b, s: (b, s, 0))

    def grouped(width):
        return [pl.BlockSpec((None, d, tm // d, width), lambda b, s: (b, 0, s, 0)) for d in DILATIONS]

    n_slots = len(DILATIONS)
    return pl.pallas_call(
        functools.partial(_out_proj_kernel, tm=tm),
        grid=(B, S // tm),
        in_specs=[tok(DN_WIDTH)] + grouped(SWA_WIDTH) + grouped(LANES) + [
            tok(D),
            pl.BlockSpec((DN_WIDTH + SWA_WIDTH, D), lambda b, s: (0, 0)),
            pl.BlockSpec((LANES, SWA_WIDTH), lambda b, s: (0, 0)),
            pl.BlockSpec((1, D), lambda b, s: (0, 0))],
        out_specs=tok(D),
        out_shape=jax.ShapeDtypeStruct((B, S, D), F32),
        scratch_shapes=[pltpu.VMEM((n_slots * SWA_WIDTH // LANES, tm, LANES), F32),
                        pltpu.VMEM((n_slots, tm, LANES), F32)],
        compiler_params=_params(2),
        name="out_proj",
    )(o_a, *outs, *lses, x, w_out, expand_mat, norm_w)


def _ffn_kernel(x_ref, xh_ref, nw_ref, wg_ref, wv_ref, cg_ref, cv_ref, bg_ref, bv_ref, wd_ref,
                pnw_ref, out_ref, h_scr, u_scr, acc_scr, *, tm, tiles_per_seq):
    halo = BF16_ROWS
    j = pl.program_id(1)

    @pl.when(j == 0)
    def _():
        x = x_ref[...]
        h_scr[halo:, :] = (x * _rms_scale(x) * nw_ref[...]).astype(BF16)
        xh = xh_ref[...]
        hh = xh * _rms_scale(xh) * nw_ref[...]
        starts_seq = (pl.program_id(0) % tiles_per_seq) == 0
        h_scr[0:halo, :] = jnp.where(starts_seq, 0.0, hh).astype(BF16)
        acc_scr[...] = jnp.zeros_like(acc_scr)

    def conv(w_ref, c_ref, b_ref):
        u_scr[...] = jnp.dot(h_scr[...], w_ref[...], preferred_element_type=F32)
        acc = u_scr[halo:halo + tm, :] * c_ref[FFN_CONV - 1:FFN_CONV, :]
        for t in range(FFN_CONV - 1):
            off = halo - (FFN_CONV - 1) + t
            acc = acc + u_scr[off:off + tm, :] * c_ref[t:t + 1, :]
        return acc + b_ref[...]

    gate = conv(wg_ref, cg_ref, bg_ref)
    cdf = 0.5 * (1.0 + jnp.tanh(math.sqrt(2.0 / math.pi) * (gate + 0.044715 * (gate * gate * gate))))
    act = (gate * cdf) * conv(wv_ref, cv_ref, bv_ref)
    acc_scr[...] += jnp.dot(act.astype(BF16), wd_ref[...], preferred_element_type=F32)

    @pl.when(j == pl.num_programs(1) - 1)
    def _():
        y = acc_scr[...]
        out_ref[...] = x_ref[...] + y * _rms_scale(y) * pnw_ref[...]


def _ffn(x, pre_w, w_up, conv_w, conv_b, w_down, post_w):
    B, S, D = x.shape
    T = B * S
    d_ff = w_down.shape[0]
    n_chunks = 2
    fc = d_ff // n_chunks
    tm = min(TOKEN_TILE, S)
    halo = BF16_ROWS
    hb = tm // halo
    xf = x.reshape(T, D)
    kern = functools.partial(_ffn_kernel, tm=tm, tiles_per_seq=S // tm)
    out = pl.pallas_call(
        kern,
        grid=(T // tm, n_chunks),
        in_specs=[
            pl.BlockSpec((tm, D), lambda i, j: (i, 0)),
            pl.BlockSpec((halo, D), lambda i, j: (jnp.maximum(i * hb - 1, 0), 0)),
            pl.BlockSpec((1, D), lambda i, j: (0, 0)),
            pl.BlockSpec((D, fc), lambda i, j: (0, j)),
            pl.BlockSpec((D, fc), lambda i, j: (0, n_chunks + j)),
            pl.BlockSpec((FFN_CONV, fc), lambda i, j: (0, j)),
            pl.BlockSpec((FFN_CONV, fc), lambda i, j: (0, n_chunks + j)),
            pl.BlockSpec((1, fc), lambda i, j: (0, j)),
            pl.BlockSpec((1, fc), lambda i, j: (0, n_chunks + j)),
            pl.BlockSpec((fc, D), lambda i, j: (j, 0)),
            pl.BlockSpec((1, D), lambda i, j: (0, 0)),
        ],
        out_specs=pl.BlockSpec((tm, D), lambda i, j: (i, 0)),
        out_shape=jax.ShapeDtypeStruct((T, D), F32),
        scratch_shapes=[
            pltpu.VMEM((tm + halo, D), BF16),
            pltpu.VMEM((tm + halo, fc), F32),
            pltpu.VMEM((tm, D), F32),
        ],
        compiler_params=_params(2),
        name="conv_glu_ffn",
    )(xf, xf, pre_w, w_up, w_up, conv_w, conv_w, conv_b, conv_b, w_down, post_w)
    return out.reshape(B, S, D)


def _rope_tables(S):
    pos = jnp.arange(S, dtype=F32)
    inv_freq = ROPE_THETA ** (-jnp.arange(0, ROPE_DIM, 2, dtype=F32) / ROPE_DIM)
    ang = pos[:, None] * inv_freq[None, :]
    cos, sin = jnp.cos(ang), jnp.sin(ang)
    pad = SWA_HEAD_DIM - ROPE_DIM
    ones = jnp.ones((S, pad), F32)
    zeros_h = jnp.zeros((S, ROPE_HALF), F32)
    zeros_p = jnp.zeros((S, pad), F32)
    cos_head = jnp.concatenate([cos, cos, ones], axis=1)
    sa_head = jnp.concatenate([-sin, zeros_h, zeros_p], axis=1)
    sb_head = jnp.concatenate([zeros_h, sin, zeros_p], axis=1)
    tile = lambda t: jnp.tile(t, (1, SWA_HEADS))
    return tile(cos_head), tile(sa_head), tile(sb_head)


def _pad_lanes(v):
    return jnp.zeros((1, LANES), F32).at[0, :v.shape[0]].set(v.astype(F32))


def kernel(x, pre_mix_norm, w_in, dn_conv, dn_a_log, dn_dt_bias, dn_out_norm, w_out, post_mix_norm,
           pre_ffn_norm, ffn_up, ffn_conv, ffn_conv_bias, ffn_down, post_ffn_norm):
    B, S, D = x.shape
    depth = w_in.shape[0]
    assert S % (max(DILATIONS) * ATTN_BLOCK) == 0 and S % TOKEN_TILE == 0
    cos_f, sin_a, sin_b = _rope_tables(S)
    qkvz = 4 * DN_WIDTH
    n_gate = 2 * DN_HEADS
    head_of_lane = jnp.arange(SWA_WIDTH) // SWA_HEAD_DIM
    expand_mat = (jnp.arange(LANES)[:, None] == head_of_lane[None, :]).astype(BF16)
    row = lambda v: v.reshape(1, -1).astype(F32)

    for l in range(depth):
        w_l = w_in[l]
        w_main = jnp.concatenate([w_l[:, :qkvz], w_l[:, qkvz + n_gate:]], axis=1).astype(BF16)
        w_gate = jnp.zeros((D, GATE_LANES), F32)
        w_gate = w_gate.at[:, :DN_HEADS].set(w_l[:, qkvz:qkvz + DN_HEADS])
        w_gate = w_gate.at[:, LANES:LANES + DN_HEADS].set(w_l[:, qkvz + DN_HEADS:qkvz + n_gate])
        proj, gates, *grouped = _in_proj(x, row(pre_mix_norm[l]), w_main, w_gate.astype(BF16),
                                         dn_conv[l].astype(F32), cos_f, sin_a, sin_b)
        o_a = _delta_mixer(proj, gates, _pad_lanes(dn_a_log[l]), _pad_lanes(dn_dt_bias[l]),
                           row(dn_out_norm[l]))
        outs, lses = zip(*[_dilated_attention_pass(src, d)
                           for src, d in zip([proj] + grouped, DILATIONS)])
        x = _out_proj(o_a, outs, lses, x, w_out[l].astype(BF16), expand_mat, row(post_mix_norm[l]))
        x = _ffn(x, row(pre_ffn_norm[l]), ffn_up[l].astype(BF16), ffn_conv[l].astype(F32),
                 row(ffn_conv_bias[l]), ffn_down[l].astype(BF16), row(post_ffn_norm[l]))
    return x
```

```python
import functools
import math

import jax
import jax.numpy as jnp
from jax import lax
from jax.experimental import pallas as pl
from jax.experimental.pallas import tpu as pltpu

F32 = jnp.float32
BF16 = jnp.bfloat16

DN_HEADS = 4
DN_HEAD_DIM = 128
DN_WIDTH = DN_HEADS * DN_HEAD_DIM
DN_CONV = 4
SWA_HEADS = 8
SWA_HEAD_DIM = 64
SWA_WIDTH = SWA_HEADS * SWA_HEAD_DIM
DILATIONS = (1, 4, 16)
DIL_STEP = 4
ATTN_BLOCK = 128
ATTN_SPAN = 128
ROPE_THETA = 500000.0
ROPE_DIM = SWA_HEAD_DIM // 4
ROPE_HALF = ROPE_DIM // 2
FFN_CONV = 3
NORM_EPS = 1e-6

LANES = 128
SUBLANES = 8
BF16_ROWS = 16
VMEM_LIMIT_BYTES = 56 * 1024 * 1024

TOKEN_TILE = 512
DELTA_BLOCK = 512
DELTA_CHUNK = 128
INV_BASE = 8
PROJ_TILE = 512
N_PROJ_TILES = 7
ATTN_TILE0 = 4
GATE_LANES = 2 * LANES


def _params(n_axes):
    return pltpu.CompilerParams(
        dimension_semantics=("arbitrary",) * n_axes, vmem_limit_bytes=VMEM_LIMIT_BYTES)


def _sigmoid(x):
    return 1.0 / (1.0 + jnp.exp(-x))


def _rms_scale(x):
    return lax.rsqrt(jnp.mean(x * x, axis=-1, keepdims=True) + NORM_EPS)


def _split3(x):
    hi = x.astype(BF16)
    r = x - hi.astype(F32)
    mid = r.astype(BF16)
    lo = (r - mid.astype(F32)).astype(BF16)
    return hi, mid, lo


def _in_proj_kernel(x_ref, xh_ref, nw_ref, w_ref, wg_ref, cw_ref, cos_ref, sa_ref, sb_ref,
                    proj_ref, gate_ref, *rest, tm):
    res_refs, conv_scr, att_scr, mid_scr = rest[:-3], rest[-3], rest[-2], rest[-1]
    groups_per_tile = PROJ_TILE // LANES
    halo = BF16_ROWS
    x = x_ref[...]
    h = (x * _rms_scale(x) * nw_ref[...]).astype(BF16)
    xh = xh_ref[...]
    hh = jnp.where(pl.program_id(0) == 0, 0.0, xh * _rms_scale(xh) * nw_ref[...]).astype(BF16)
    h_ext = jnp.concatenate([hh, h], axis=0)
    gate_ref[...] = jnp.dot(h, wg_ref[...], preferred_element_type=F32)

    def conv_silu_norm(j, cols):
        conv_scr[j] = jnp.dot(h_ext, w_ref[:, cols], preferred_element_type=F32)
        acc = conv_scr[j, halo:halo + tm, :] * cw_ref[DN_CONV - 1:DN_CONV, cols]
        for t in range(DN_CONV - 1):
            off = halo - (DN_CONV - 1) + t
            acc = acc + conv_scr[j, off:off + tm, :] * cw_ref[t:t + 1, cols]
        y = acc * _sigmoid(acc)
        if j == 2:
            return y
        scale = DN_HEAD_DIM ** -0.5 if j == 0 else 1.0
        heads = []
        for hd in range(DN_HEADS):
            t = y[:, hd * DN_HEAD_DIM:(hd + 1) * DN_HEAD_DIM]
            heads.append(t * (lax.rsqrt(jnp.sum(t * t, axis=-1, keepdims=True) + NORM_EPS) * scale))
        return jnp.concatenate(heads, axis=1)

    def regroup(planes):
        src_scr, rows_per_group = att_scr, tm
        for level, res_ref in enumerate(res_refs):
            n_out = rows_per_group // DIL_STEP
            for grp in range(DIL_STEP ** level):
                for step in range(DIL_STEP):
                    r = step * DIL_STEP ** level + grp
                    start = grp * rows_per_group + step
                    for g in planes:
                        t = src_scr[g, pl.ds(start, n_out, stride=DIL_STEP), :]
                        res_ref[r, :, g * LANES:(g + 1) * LANES] = t.astype(BF16)
                        if level + 1 < len(res_refs):
                            mid_scr[g, r * n_out:(r + 1) * n_out, :] = t
            src_scr, rows_per_group = mid_scr, n_out

    tile_order = list(range(ATTN_TILE0, N_PROJ_TILES)) + list(range(ATTN_TILE0))
    for j in tile_order:
        if j < N_PROJ_TILES - ATTN_TILE0:
            regroup(range(j * groups_per_tile, (j + 1) * groups_per_tile))
        cols = slice(j * PROJ_TILE, (j + 1) * PROJ_TILE)
        if j < 3:
            proj_ref[:, cols] = conv_silu_norm(j, cols).astype(BF16)
            continue
        y = jnp.dot(h, w_ref[:, cols], preferred_element_type=F32)
        if j in (4, 5):
            if j == 4:
                y = y * (SWA_HEAD_DIM ** -0.5)
            y = (y * cos_ref[...]
                 + pltpu.roll(y, PROJ_TILE - ROPE_HALF, 1) * sa_ref[...]
                 + pltpu.roll(y, ROPE_HALF, 1) * sb_ref[...])
        proj_ref[:, cols] = y.astype(BF16)
        if j >= ATTN_TILE0:
            for g in range(groups_per_tile):
                att_scr[(j - ATTN_TILE0) * groups_per_tile + g] = y[:, g * LANES:(g + 1) * LANES]


def _in_proj(x, norm_w, w_main, w_gate, conv_w, cos_f, sin_a, sin_b):
    B, S, D = x.shape
    tm = min(TOKEN_TILE, S)
    halo = BF16_ROWS
    hb = tm // halo
    n_cols = N_PROJ_TILES * PROJ_TILE
    att_cols = 3 * SWA_WIDTH
    res_specs = [pl.BlockSpec((None, d, tm // d, att_cols), lambda s, b: (b, 0, s, 0))
                 for d in DILATIONS[1:]]
    res_shapes = [jax.ShapeDtypeStruct((B, d, S // d, att_cols), BF16) for d in DILATIONS[1:]]
    return pl.pallas_call(
        functools.partial(_in_proj_kernel, tm=tm),
        grid=(S // tm, B),
        in_specs=[
            pl.BlockSpec((None, tm, D), lambda s, b: (b, s, 0)),
            pl.BlockSpec((None, halo, D), lambda s, b: (b, jnp.maximum(s * hb - 1, 0), 0)),
            pl.BlockSpec((1, D), lambda s, b: (0, 0)),
            pl.BlockSpec((D, n_cols), lambda s, b: (0, 0)),
            pl.BlockSpec((D, GATE_LANES), lambda s, b: (0, 0)),
            pl.BlockSpec((DN_CONV, 3 * DN_WIDTH), lambda s, b: (0, 0)),
            pl.BlockSpec((tm, PROJ_TILE), lambda s, b: (s, 0)),
            pl.BlockSpec((tm, PROJ_TILE), lambda s, b: (s, 0)),
            pl.BlockSpec((tm, PROJ_TILE), lambda s, b: (s, 0)),
        ],
        out_specs=[
            pl.BlockSpec((None, tm, n_cols), lambda s, b: (b, s, 0)),
            pl.BlockSpec((None, tm, GATE_LANES), lambda s, b: (b, s, 0)),
        ] + res_specs,
        out_shape=[
            jax.ShapeDtypeStruct((B, S, n_cols), BF16),
            jax.ShapeDtypeStruct((B, S, GATE_LANES), F32),
        ] + res_shapes,
        scratch_shapes=[pltpu.VMEM((3, tm + halo, PROJ_TILE), F32)]
        + [pltpu.VMEM((att_cols // LANES, tm, LANES), F32)] * 2,
        compiler_params=_params(2),
        name="in_proj",
    )(x, x, norm_w, w_main, w_gate, conv_w, cos_f, sin_a, sin_b)


def _delta_kernel(qkv_ref, z_ref, gate_ref, alog_ref, dtb_ref, onw_ref, o_ref, state_scr,
                  *, tb, chunk):
    @pl.when(pl.program_id(1) == 0)
    def _():
        state_scr[...] = jnp.zeros_like(state_scr)

    gates = gate_ref[...]
    beta_all = _sigmoid(gates[:, :LANES])
    a_in = gates[:, LANES:] + dtb_ref[...]
    softplus = jnp.maximum(a_in, 0.0) + jnp.log(1.0 + jnp.exp(-jnp.abs(a_in)))
    g_all = -jnp.exp(alog_ref[...]) * softplus
    g_parts = _split3(g_all)

    r_i = lax.broadcasted_iota(jnp.int32, (chunk, chunk), 0)
    c_i = lax.broadcasted_iota(jnp.int32, (chunk, chunk), 1)
    incl = r_i >= c_i
    strict = r_i > c_i
    tri = jnp.where(incl, 1.0, 0.0).astype(BF16)
    eye = jnp.where(r_i == c_i, 1.0, 0.0).astype(F32)

    def same_block(log_size):
        return (r_i >> log_size) == (c_i >> log_size)

    log_s = int(math.log2(INV_BASE))
    level_masks = [same_block(log_s)]
    while (1 << log_s) < chunk:
        level_masks.append(same_block(log_s + 1) & (((r_i >> log_s) & 1) == 1)
                           & (((c_i >> log_s) & 1) == 0))
        log_s += 1

    def head_cols(part, h):
        return slice(part * DN_WIDTH + h * DN_HEAD_DIM, part * DN_WIDTH + (h + 1) * DN_HEAD_DIM)

    def mm(a, b):
        return jnp.dot(a.astype(BF16), b.astype(BF16), preferred_element_type=F32)

    n_chunks = tb // chunk
    probs = [(c, h) for c in range(n_chunks) for h in range(DN_HEADS)]
    row_of = lambda c: slice(c * chunk, (c + 1) * chunk)

    Gs = [sum(jnp.dot(tri, p[row_of(c)], preferred_element_type=F32) for p in g_parts)
          for c in range(n_chunks)]
    G_ts = [G.T for G in Gs]

    q_l = [qkv_ref[row_of(c), head_cols(0, h)].astype(F32) for c, h in probs]
    k_l = [qkv_ref[row_of(c), head_cols(1, h)].astype(F32) for c, h in probs]
    v_l = [qkv_ref[row_of(c), head_cols(2, h)].astype(F32) for c, h in probs]
    gcol_l = [Gs[c][:, h:h + 1] for c, h in probs]
    grow_l = [G_ts[c][h:h + 1, :] for c, h in probs]
    glast_l = [g[:, chunk - 1:chunk] for g in grow_l]
    bcol_l = [beta_all[row_of(c), h:h + 1] for c, h in probs]
    gamma_l = [jnp.exp(jnp.where(incl, gc - gr, -jnp.inf)) for gc, gr in zip(gcol_l, grow_l)]
    eg_l = [jnp.exp(g) for g in gcol_l]
    kb_l = [k * b for k, b in zip(k_l, bcol_l)]
    kkqk_l = [lax.dot_general(jnp.concatenate([kb, q], axis=0).astype(BF16), k.astype(BF16),
                              (((1,), (1,)), ((), ())), preferred_element_type=F32)
              for kb, q, k in zip(kb_l, q_l, k_l)]
    a_l = [jnp.where(strict, kk[:chunk] * gm, 0.0) for kk, gm in zip(kkqk_l, gamma_l)]
    qk_l = [kk[chunk:] * gm for kk, gm in zip(kkqk_l, gamma_l)]

    p_l = [jnp.where(level_masks[0], -a, 0.0) for a in a_l]
    t_l = [eye + x for x in p_l]
    for _ in range(int(math.log2(INV_BASE)) - 1):
        p_l = [mm(p, p) for p in p_l]
        t_l = [t + mm(p, t) for p, t in zip(p_l, t_l)]
    for lvl_mask in level_masks[1:]:
        tb_l = [t.astype(BF16) for t in t_l]
        ld_l = [jnp.dot(jnp.where(lvl_mask, a, 0.0).astype(BF16), t_b, preferred_element_type=F32)
                for a, t_b in zip(a_l, tb_l)]
        t_l = [t - jnp.dot(t_b, ld.astype(BF16), preferred_element_type=F32)
               for t, t_b, ld in zip(t_l, tb_l, ld_l)]
    sol_l = [mm(t, jnp.concatenate([v * b, kb * eg], axis=1))
             for t, v, b, kb, eg in zip(t_l, v_l, bcol_l, kb_l, eg_l)]
    wq_l = [jnp.concatenate([sol[:, DN_HEAD_DIM:], q * eg], axis=0).astype(BF16)
            for sol, q, eg in zip(sol_l, q_l, eg_l)]
    kdt_l = [(k * jnp.exp(gl - gc)).T.astype(BF16) for k, gl, gc in zip(k_l, glast_l, gcol_l)]

    states = [state_scr[h] for h in range(DN_HEADS)]
    for c in range(n_chunks):
        ids = [c * DN_HEADS + h for h in range(DN_HEADS)]
        ws_l = [jnp.dot(wq_l[i], states[h].astype(BF16), preferred_element_type=F32)
                for h, i in enumerate(ids)]
        vnew_l = [(sol_l[i][:, :DN_HEAD_DIM] - ws[:chunk]).astype(BF16) for i, ws in zip(ids, ws_l)]
        o_l = [ws[chunk:] + jnp.dot(qk_l[i].astype(BF16), vn, preferred_element_type=F32)
               for i, ws, vn in zip(ids, ws_l, vnew_l)]
        states = [states[h] * jnp.exp(glast_l[i]) + jnp.dot(kdt_l[i], vn, preferred_element_type=F32)
                  for (h, i), vn in zip(enumerate(ids), vnew_l)]
        for h, o in enumerate(o_l):
            cols = slice(h * DN_HEAD_DIM, (h + 1) * DN_HEAD_DIM)
            zf = z_ref[row_of(c), cols].astype(F32)
            o = o * _rms_scale(o) * onw_ref[...] * (zf * _sigmoid(zf))
            o_ref[row_of(c), cols] = o.astype(BF16)
    for h in range(DN_HEADS):
        state_scr[h] = states[h]


def _delta_mixer(proj, gates, a_log, dt_bias, out_norm_w):
    B, S, _ = proj.shape
    tb = min(DELTA_BLOCK, S)
    chunk = min(DELTA_CHUNK, tb)
    kern = functools.partial(_delta_kernel, tb=tb, chunk=chunk)
    return pl.pallas_call(
        kern,
        grid=(B, S // tb),
        in_specs=[
            pl.BlockSpec((None, tb, 3 * DN_WIDTH), lambda b, s: (b, s, 0)),
            pl.BlockSpec((None, tb, DN_WIDTH), lambda b, s: (b, s, 3)),
            pl.BlockSpec((None, tb, GATE_LANES), lambda b, s: (b, s, 0)),
            pl.BlockSpec((1, LANES), lambda b, s: (0, 0)),
            pl.BlockSpec((1, LANES), lambda b, s: (0, 0)),
            pl.BlockSpec((1, DN_HEAD_DIM), lambda b, s: (0, 0)),
        ],
        out_specs=pl.BlockSpec((None, tb, DN_WIDTH), lambda b, s: (b, s, 0)),
        out_shape=jax.ShapeDtypeStruct((B, S, DN_WIDTH), BF16),
        scratch_shapes=[pltpu.VMEM((DN_HEADS, DN_HEAD_DIM, DN_HEAD_DIM), F32)],
        compiler_params=_params(2),
        name="delta_mixer",
    )(proj, proj, gates, a_log, dt_bias, out_norm_w)


def _attn_kernel(q_ref, kp_ref, kc_ref, vp_ref, vc_ref, o_ref, lse_ref, *, bq):
    blk = ATTN_BLOCK
    q_i = lax.broadcasted_iota(jnp.int32, (blk, 2 * blk), 0)
    k_i = lax.broadcasted_iota(jnp.int32, (blk, 2 * blk), 1)
    band = (k_i >= q_i) & (k_i <= q_i + ATTN_SPAN)
    first_lo = jnp.where(pl.program_id(2) == 0, blk, 0)
    band_first = band & (k_i >= first_lo)
    lane = lax.broadcasted_iota(jnp.int32, (blk, LANES), 1)
    low_half = lane < SWA_HEAD_DIM

    for s in range(bq // blk):
        rows = slice(s * blk, (s + 1) * blk)
        q = q_ref[rows, :]
        if s == 0:
            k = jnp.concatenate([kp_ref[...], kc_ref[0:blk, :]], axis=0)
            v = jnp.concatenate([vp_ref[...], vc_ref[0:blk, :]], axis=0)
            valid = band_first
        else:
            k = kc_ref[(s - 1) * blk:(s + 1) * blk, :]
            v = vc_ref[(s - 1) * blk:(s + 1) * blk, :]
            valid = band
        lse_acc = jnp.zeros((blk, LANES), F32)
        for pair in range(SWA_HEADS // 2):
            cols = slice(pair * LANES, (pair + 1) * LANES)
            q2, k2, v2 = q[:, cols], k[:, cols], v[:, cols]
            halves = []
            for half in range(2):
                mask = low_half if half == 0 else jnp.logical_not(low_half)
                qh = jnp.where(mask, q2, jnp.zeros_like(q2))
                sc = lax.dot_general(qh, k2, (((1,), (1,)), ((), ())), preferred_element_type=F32)
                sc = jnp.where(valid, sc, -jnp.inf)
                m = jnp.max(sc, axis=-1, keepdims=True)
                p = jnp.exp(sc - m)
                l = jnp.sum(p, axis=-1, keepdims=True)
                halves.append(jnp.dot(p.astype(BF16), v2, preferred_element_type=F32) / l)
                lse_acc = jnp.where(lane == 2 * pair + half, m + jnp.log(l), lse_acc)
            o_ref[rows, cols] = jnp.where(low_half, halves[0], halves[1]).astype(BF16)
        lse_ref[rows, :] = lse_acc


def _dilated_attention_pass(src, dil):
    blk = ATTN_BLOCK
    if dil == 1:
        B, L, _ = src.shape
        tile0 = ATTN_TILE0
        src = src.reshape(B, 1, L, src.shape[-1])
    else:
        B, _, L, _ = src.shape
        tile0 = 0
    bq = min(512, L)
    sub = bq // blk
    kern = functools.partial(_attn_kernel, bq=bq)

    def cur(col):
        return pl.BlockSpec((None, None, bq, PROJ_TILE), lambda b, r, i: (b, r, i, tile0 + col))

    def prev(col):
        return pl.BlockSpec((None, None, blk, PROJ_TILE),
                            lambda b, r, i: (b, r, jnp.maximum(i * sub - 1, 0), tile0 + col))

    return pl.pallas_call(
        kern,
        grid=(B, dil, L // bq),
        in_specs=[cur(0), prev(1), cur(1), prev(2), cur(2)],
        out_specs=[
            pl.BlockSpec((None, None, bq, SWA_WIDTH), lambda b, r, i: (b, r, i, 0)),
            pl.BlockSpec((None, None, bq, LANES), lambda b, r, i: (b, r, i, 0)),
        ],
        out_shape=[
            jax.ShapeDtypeStruct((B, dil, L, SWA_WIDTH), BF16),
            jax.ShapeDtypeStruct((B, dil, L, LANES), F32),
        ],
        compiler_params=_params(3),
        name=f"dilated_attn_d{dil}",
    )(src, src, src, src, src)


def _out_proj_kernel(oa_ref, o1_ref, o2_ref, o3_ref, l1_ref, l2_ref, l3_ref, x_ref, w_ref,
                     e_ref, nw_ref, out_ref, o_scr, l_scr, *, tm):
    n_planes = SWA_WIDTH // LANES
    tmp_slot = len(DILATIONS) - 1

    def token_order(o_ref, l_ref, level):
        if level == 0:
            return o_ref[0].astype(F32), l_ref[0]

        def from_input(r):
            o_res = o_ref[r].astype(F32)
            return [o_res[:, g * LANES:(g + 1) * LANES] for g in range(n_planes)], l_ref[r]

        get, rows = from_input, tm // DIL_STEP ** level
        for lvl in range(level, 0, -1):
            dst = level - 1 if lvl == 1 else tmp_slot
            for grp in range(DIL_STEP ** (lvl - 1)):
                for step in range(DIL_STEP):
                    planes, l_res = get(step * DIL_STEP ** (lvl - 1) + grp)
                    dst_rows = pl.ds(grp * rows * DIL_STEP + step, rows, stride=DIL_STEP)
                    for g in range(n_planes):
                        o_scr[dst * n_planes + g, dst_rows, :] = planes[g]
                    l_scr[dst, dst_rows, :] = l_res
            rows *= DIL_STEP

            def get(r, src=dst, n=rows):
                blk = slice(r * n, (r + 1) * n)
                return [o_scr[src * n_planes + g, blk, :] for g in range(n_planes)], l_scr[src, blk, :]

        planes, l_tok = get(0)
        return jnp.concatenate(planes, axis=1), l_tok

    pairs = [token_order(o, l, lvl) for lvl, (o, l) in enumerate(
        zip((o1_ref, o2_ref, o3_ref), (l1_ref, l2_ref, l3_ref)))]
    lses = [l for _, l in pairs]
    m = jnp.maximum(jnp.maximum(lses[0], lses[1]), lses[2])
    es = [jnp.exp(l - m) for l in lses]
    den = es[0] + es[1] + es[2]

    def expand(wt):
        hi = wt.astype(BF16)
        lo = (wt - hi.astype(F32)).astype(BF16)
        return jnp.dot(jnp.concatenate([hi, lo], axis=1), e_ref[...], preferred_element_type=F32)

    ob = None
    for e, (o, _) in zip(es, pairs):
        term = expand(e / den) * o
        ob = term if ob is None else ob + term
    y = (jnp.dot(oa_ref[...], w_ref[0:DN_WIDTH, :], preferred_element_type=F32)
         + jnp.dot(ob.astype(BF16), w_ref[DN_WIDTH:, :], preferred_element_type=F32))
    out_ref[...] = x_ref[...] + y * _rms_scale(y) * nw_ref[...]


def _out_proj(o_a, outs, lses, x, w_out, expand_mat, norm_w):
    B, S, D = x.shape
    tm = min(TOKEN_TILE, S)

    def tok(width):
        return pl.BlockSpec((None, tm, width), lambda b, s: (b, s, 0))

    def grouped(width):
        return [pl.BlockSpec((None, d, tm // d, width), lambda b, s: (b, 0, s, 0)) for d in DILATIONS]

    n_slots = len(DILATIONS)
    return pl.pallas_call(
        functools.partial(_out_proj_kernel, tm=tm),
        grid=(B, S // tm),
        in_specs=[tok(DN_WIDTH)] + grouped(SWA_WIDTH) + grouped(LANES) + [
            tok(D),
            pl.BlockSpec((DN_WIDTH + SWA_WIDTH, D), lambda b, s: (0, 0)),
            pl.BlockSpec((2 * LANES, SWA_WIDTH), lambda b, s: (0, 0)),
            pl.BlockSpec((1, D), lambda b, s: (0, 0))],
        out_specs=tok(D),
        out_shape=jax.ShapeDtypeStruct((B, S, D), F32),
        scratch_shapes=[pltpu.VMEM((n_slots * SWA_WIDTH // LANES, tm, LANES), F32),
                        pltpu.VMEM((n_slots, tm, LANES), F32)],
        compiler_params=_params(2),
        name="out_proj",
    )(o_a, *outs, *lses, x, w_out, expand_mat, norm_w)


def _ffn_kernel(x_ref, xh_ref, nw_ref, wg_ref, wv_ref, cg_ref, cv_ref, bg_ref, bv_ref, wd_ref,
                pnw_ref, out_ref, h_scr, u_scr, *, tm, tiles_per_seq):
    halo = BF16_ROWS
    x = x_ref[...]
    h_scr[halo:, :] = (x * _rms_scale(x) * nw_ref[...]).astype(BF16)
    xh = xh_ref[...]
    hh = xh * _rms_scale(xh) * nw_ref[...]
    starts_seq = (pl.program_id(0) % tiles_per_seq) == 0
    h_scr[0:halo, :] = jnp.where(starts_seq, 0.0, hh).astype(BF16)

    def conv(w_ref, c_ref, b_ref):
        u_scr[...] = jnp.dot(h_scr[...], w_ref[...], preferred_element_type=F32)
        acc = u_scr[halo:halo + tm, :] * c_ref[FFN_CONV - 1:FFN_CONV, :]
        for t in range(FFN_CONV - 1):
            off = halo - (FFN_CONV - 1) + t
            acc = acc + u_scr[off:off + tm, :] * c_ref[t:t + 1, :]
        return acc + b_ref[...]

    gate = conv(wg_ref, cg_ref, bg_ref)
    cdf = 0.5 * (1.0 + jnp.tanh(math.sqrt(2.0 / math.pi) * (gate + 0.044715 * (gate * gate * gate))))
    act = (gate * cdf) * conv(wv_ref, cv_ref, bv_ref)
    y = jnp.dot(act.astype(BF16), wd_ref[...], preferred_element_type=F32)
    out_ref[...] = x + y * _rms_scale(y) * pnw_ref[...]


def _ffn(x, pre_w, w_up, conv_w, conv_b, w_down, post_w):
    B, S, D = x.shape
    T = B * S
    d_ff = w_down.shape[0]
    tm = min(TOKEN_TILE, S)
    halo = BF16_ROWS
    hb = tm // halo
    xf = x.reshape(T, D)
    kern = functools.partial(_ffn_kernel, tm=tm, tiles_per_seq=S // tm)
    out = pl.pallas_call(
        kern,
        grid=(T // tm,),
        in_specs=[
            pl.BlockSpec((tm, D), lambda i: (i, 0)),
            pl.BlockSpec((halo, D), lambda i: (jnp.maximum(i * hb - 1, 0), 0)),
            pl.BlockSpec((1, D), lambda i: (0, 0)),
            pl.BlockSpec((D, d_ff), lambda i: (0, 0)),
            pl.BlockSpec((D, d_ff), lambda i: (0, 1)),
            pl.BlockSpec((FFN_CONV, d_ff), lambda i: (0, 0)),
            pl.BlockSpec((FFN_CONV, d_ff), lambda i: (0, 1)),
            pl.BlockSpec((1, d_ff), lambda i: (0, 0)),
            pl.BlockSpec((1, d_ff), lambda i: (0, 1)),
            pl.BlockSpec((d_ff, D), lambda i: (0, 0)),
            pl.BlockSpec((1, D), lambda i: (0, 0)),
        ],
        out_specs=pl.BlockSpec((tm, D), lambda i: (i, 0)),
        out_shape=jax.ShapeDtypeStruct((T, D), F32),
        scratch_shapes=[
            pltpu.VMEM((tm + halo, D), BF16),
            pltpu.VMEM((tm + halo, d_ff), F32),
        ],
        compiler_params=_params(1),
        name="conv_glu_ffn",
    )(xf, xf, pre_w, w_up, w_up, conv_w, conv_w, conv_b, conv_b, w_down, post_w)
    return out.reshape(B, S, D)


def _rope_tables(S):
    pos = jnp.arange(S, dtype=F32)
    inv_freq = ROPE_THETA ** (-jnp.arange(0, ROPE_DIM, 2, dtype=F32) / ROPE_DIM)
    ang = pos[:, None] * inv_freq[None, :]
    cos, sin = jnp.cos(ang), jnp.sin(ang)
    pad = SWA_HEAD_DIM - ROPE_DIM
    ones = jnp.ones((S, pad), F32)
    zeros_h = jnp.zeros((S, ROPE_HALF), F32)
    zeros_p = jnp.zeros((S, pad), F32)
    cos_head = jnp.concatenate([cos, cos, ones], axis=1)
    sa_head = jnp.concatenate([-sin, zeros_h, zeros_p], axis=1)
    sb_head = jnp.concatenate([zeros_h, sin, zeros_p], axis=1)
    tile = lambda t: jnp.tile(t, (1, SWA_HEADS))
    return tile(cos_head), tile(sa_head), tile(sb_head)


def _pad_lanes(v):
    return jnp.zeros((1, LANES), F32).at[0, :v.shape[0]].set(v.astype(F32))


def kernel(x, pre_mix_norm, w_in, dn_conv, dn_a_log, dn_dt_bias, dn_out_norm, w_out, post_mix_norm,
           pre_ffn_norm, ffn_up, ffn_conv, ffn_conv_bias, ffn_down, post_ffn_norm):
    B, S, D = x.shape
    depth = w_in.shape[0]
    assert S % (max(DILATIONS) * ATTN_BLOCK) == 0 and S % TOKEN_TILE == 0
    cos_f, sin_a, sin_b = _rope_tables(S)
    qkvz = 4 * DN_WIDTH
    n_gate = 2 * DN_HEADS
    head_of_lane = jnp.arange(SWA_WIDTH) // SWA_HEAD_DIM
    expand_mat = (jnp.arange(2 * LANES)[:, None] % LANES == head_of_lane[None, :]).astype(BF16)
    row = lambda v: v.reshape(1, -1).astype(F32)

    for l in range(depth):
        w_l = w_in[l]
        w_main = jnp.concatenate([w_l[:, :qkvz], w_l[:, qkvz + n_gate:]], axis=1).astype(BF16)
        w_gate = jnp.zeros((D, GATE_LANES), F32)
        w_gate = w_gate.at[:, :DN_HEADS].set(w_l[:, qkvz:qkvz + DN_HEADS])
        w_gate = w_gate.at[:, LANES:LANES + DN_HEADS].set(w_l[:, qkvz + DN_HEADS:qkvz + n_gate])
        proj, gates, *grouped = _in_proj(x, row(pre_mix_norm[l]), w_main, w_gate.astype(BF16),
                                         dn_conv[l].astype(F32), cos_f, sin_a, sin_b)
        o_a = _delta_mixer(proj, gates, _pad_lanes(dn_a_log[l]), _pad_lanes(dn_dt_bias[l]),
                           row(dn_out_norm[l]))
        outs, lses = zip(*[_dilated_attention_pass(src, d)
                           for src, d in zip([proj] + grouped, DILATIONS)])
        x = _out_proj(o_a, outs, lses, x, w_out[l].astype(BF16), expand_mat, row(post_mix_norm[l]))
        x = _ffn(x, row(pre_ffn_norm[l]), ffn_up[l].astype(BF16), ffn_conv[l].astype(F32),
                 row(ffn_conv_bias[l]), ffn_down[l].astype(BF16), row(post_ffn_norm[l]))
    return x
```

```python
import functools
import math

import jax
import jax.numpy as jnp
from jax import lax
from jax.experimental import pallas as pl
from jax.experimental.pallas import tpu as pltpu

F32 = jnp.float32
BF16 = jnp.bfloat16

DN_HEADS = 4
DN_HEAD_DIM = 128
DN_WIDTH = DN_HEADS * DN_HEAD_DIM
DN_CONV = 4
SWA_HEADS = 8
SWA_HEAD_DIM = 64
SWA_WIDTH = SWA_HEADS * SWA_HEAD_DIM
DILATIONS = (1, 4, 16)
DIL_STEP = 4
ATTN_BLOCK = 128
ATTN_SPAN = 128
ROPE_THETA = 500000.0
ROPE_DIM = SWA_HEAD_DIM // 4
ROPE_HALF = ROPE_DIM // 2
FFN_CONV = 3
NORM_EPS = 1e-6

LANES = 128
SUBLANES = 8
BF16_ROWS = 16
VMEM_LIMIT_BYTES = 56 * 1024 * 1024

TOKEN_TILE = 512
DELTA_BLOCK = 512
DELTA_CHUNK = 128
INV_BASE = 8
PROJ_TILE = 512
N_PROJ_TILES = 7
ATTN_TILE0 = 4
GATE_LANES = 2 * LANES
ATTN_GROUP = 1


def _params(n_axes):
    return pltpu.CompilerParams(
        dimension_semantics=("arbitrary",) * n_axes, vmem_limit_bytes=VMEM_LIMIT_BYTES)


def _sigmoid(x):
    return 1.0 / (1.0 + jnp.exp(-x))


def _rms_scale(x):
    return lax.rsqrt(jnp.mean(x * x, axis=-1, keepdims=True) + NORM_EPS)


def _split3(x):
    hi = x.astype(BF16)
    r = x - hi.astype(F32)
    mid = r.astype(BF16)
    lo = (r - mid.astype(F32)).astype(BF16)
    return hi, mid, lo


def _in_proj_kernel(x_ref, xh_ref, nw_ref, w_ref, wg_ref, cw_ref, cos_ref, sa_ref, sb_ref,
                    proj_ref, gate_ref, *rest, tm):
    res_refs, conv_scr, att_scr, mid_scr = rest[:-3], rest[-3], rest[-2], rest[-1]
    groups_per_tile = PROJ_TILE // LANES
    halo = BF16_ROWS
    x = x_ref[...]
    h = (x * _rms_scale(x) * nw_ref[...]).astype(BF16)
    xh = xh_ref[...]
    hh = jnp.where(pl.program_id(0) == 0, 0.0, xh * _rms_scale(xh) * nw_ref[...]).astype(BF16)
    h_ext = jnp.concatenate([hh, h], axis=0)
    gate_ref[...] = jnp.dot(h, wg_ref[...], preferred_element_type=F32)

    def conv_silu_norm(j, cols):
        conv_scr[j] = jnp.dot(h_ext, w_ref[:, cols], preferred_element_type=F32)
        acc = conv_scr[j, halo:halo + tm, :] * cw_ref[DN_CONV - 1:DN_CONV, cols]
        for t in range(DN_CONV - 1):
            off = halo - (DN_CONV - 1) + t
            acc = acc + conv_scr[j, off:off + tm, :] * cw_ref[t:t + 1, cols]
        y = acc * _sigmoid(acc)
        if j == 2:
            return y
        scale = DN_HEAD_DIM ** -0.5 if j == 0 else 1.0
        heads = []
        for hd in range(DN_HEADS):
            t = y[:, hd * DN_HEAD_DIM:(hd + 1) * DN_HEAD_DIM]
            heads.append(t * (lax.rsqrt(jnp.sum(t * t, axis=-1, keepdims=True) + NORM_EPS) * scale))
        return jnp.concatenate(heads, axis=1)

    def regroup(planes):
        src_scr, rows_per_group = att_scr, tm
        for level, res_ref in enumerate(res_refs):
            n_out = rows_per_group // DIL_STEP
            for grp in range(DIL_STEP ** level):
                for step in range(DIL_STEP):
                    r = step * DIL_STEP ** level + grp
                    start = grp * rows_per_group + step
                    for g in planes:
                        t = src_scr[g, pl.ds(start, n_out, stride=DIL_STEP), :]
                        res_ref[r, :, g * LANES:(g + 1) * LANES] = t.astype(BF16)
                        if level + 1 < len(res_refs):
                            mid_scr[g, r * n_out:(r + 1) * n_out, :] = t
            src_scr, rows_per_group = mid_scr, n_out

    tile_order = list(range(ATTN_TILE0, N_PROJ_TILES)) + list(range(ATTN_TILE0))
    for j in tile_order:
        if j < N_PROJ_TILES - ATTN_TILE0:
            regroup(range(j * groups_per_tile, (j + 1) * groups_per_tile))
        cols = slice(j * PROJ_TILE, (j + 1) * PROJ_TILE)
        if j < 3:
            proj_ref[:, cols] = conv_silu_norm(j, cols).astype(BF16)
            continue
        y = jnp.dot(h, w_ref[:, cols], preferred_element_type=F32)
        if j in (4, 5):
            if j == 4:
                y = y * (SWA_HEAD_DIM ** -0.5)
            y = (y * cos_ref[...]
                 + pltpu.roll(y, PROJ_TILE - ROPE_HALF, 1) * sa_ref[...]
                 + pltpu.roll(y, ROPE_HALF, 1) * sb_ref[...])
        proj_ref[:, cols] = y.astype(BF16)
        if j >= ATTN_TILE0:
            for g in range(groups_per_tile):
                att_scr[(j - ATTN_TILE0) * groups_per_tile + g] = y[:, g * LANES:(g + 1) * LANES]


def _in_proj(x, norm_w, w_main, w_gate, conv_w, cos_f, sin_a, sin_b):
    B, S, D = x.shape
    tm = min(TOKEN_TILE, S)
    halo = BF16_ROWS
    hb = tm // halo
    n_cols = N_PROJ_TILES * PROJ_TILE
    att_cols = 3 * SWA_WIDTH
    res_specs = [pl.BlockSpec((None, d, tm // d, att_cols), lambda s, b: (b, 0, s, 0))
                 for d in DILATIONS[1:]]
    res_shapes = [jax.ShapeDtypeStruct((B, d, S // d, att_cols), BF16) for d in DILATIONS[1:]]
    return pl.pallas_call(
        functools.partial(_in_proj_kernel, tm=tm),
        grid=(S // tm, B),
        in_specs=[
            pl.BlockSpec((None, tm, D), lambda s, b: (b, s, 0)),
            pl.BlockSpec((None, halo, D), lambda s, b: (b, jnp.maximum(s * hb - 1, 0), 0)),
            pl.BlockSpec((1, D), lambda s, b: (0, 0)),
            pl.BlockSpec((D, n_cols), lambda s, b: (0, 0)),
            pl.BlockSpec((D, GATE_LANES), lambda s, b: (0, 0)),
            pl.BlockSpec((DN_CONV, 3 * DN_WIDTH), lambda s, b: (0, 0)),
            pl.BlockSpec((tm, PROJ_TILE), lambda s, b: (s, 0)),
            pl.BlockSpec((tm, PROJ_TILE), lambda s, b: (s, 0)),
            pl.BlockSpec((tm, PROJ_TILE), lambda s, b: (s, 0)),
        ],
        out_specs=[
            pl.BlockSpec((None, tm, n_cols), lambda s, b: (b, s, 0)),
            pl.BlockSpec((None, tm, GATE_LANES), lambda s, b: (b, s, 0)),
        ] + res_specs,
        out_shape=[
            jax.ShapeDtypeStruct((B, S, n_cols), BF16),
            jax.ShapeDtypeStruct((B, S, GATE_LANES), F32),
        ] + res_shapes,
        scratch_shapes=[pltpu.VMEM((3, tm + halo, PROJ_TILE), F32)]
        + [pltpu.VMEM((att_cols // LANES, tm, LANES), F32)] * 2,
        compiler_params=_params(2),
        name="in_proj",
    )(x, x, norm_w, w_main, w_gate, conv_w, cos_f, sin_a, sin_b)


def _delta_kernel(qkv_ref, z_ref, gate_ref, alog_ref, dtb_ref, onw_ref, o_ref, state_scr,
                  *, tb, chunk):
    @pl.when(pl.program_id(1) == 0)
    def _():
        state_scr[...] = jnp.zeros_like(state_scr)

    gates = gate_ref[...]
    beta_all = _sigmoid(gates[:, :LANES])
    a_in = gates[:, LANES:] + dtb_ref[...]
    softplus = jnp.maximum(a_in, 0.0) + jnp.log(1.0 + jnp.exp(-jnp.abs(a_in)))
    g_all = -jnp.exp(alog_ref[...]) * softplus
    g_parts = _split3(g_all)

    r_i = lax.broadcasted_iota(jnp.int32, (chunk, chunk), 0)
    c_i = lax.broadcasted_iota(jnp.int32, (chunk, chunk), 1)
    incl = r_i >= c_i
    strict = r_i > c_i
    tri = jnp.where(incl, 1.0, 0.0).astype(BF16)
    eye = jnp.where(r_i == c_i, 1.0, 0.0).astype(F32)

    def same_block(log_size):
        return (r_i >> log_size) == (c_i >> log_size)

    log_s = int(math.log2(INV_BASE))
    level_masks = [same_block(log_s)]
    while (1 << log_s) < chunk:
        level_masks.append(same_block(log_s + 1) & (((r_i >> log_s) & 1) == 1)
                           & (((c_i >> log_s) & 1) == 0))
        log_s += 1

    def head_cols(part, h):
        return slice(part * DN_WIDTH + h * DN_HEAD_DIM, part * DN_WIDTH + (h + 1) * DN_HEAD_DIM)

    def mm(a, b):
        return jnp.dot(a.astype(BF16), b.astype(BF16), preferred_element_type=F32)

    n_chunks = tb // chunk
    probs = [(c, h) for c in range(n_chunks) for h in range(DN_HEADS)]
    row_of = lambda c: slice(c * chunk, (c + 1) * chunk)

    Gs = [sum(jnp.dot(tri, p[row_of(c)], preferred_element_type=F32) for p in g_parts)
          for c in range(n_chunks)]
    G_ts = [G.T for G in Gs]

    q_l = [qkv_ref[row_of(c), head_cols(0, h)].astype(F32) for c, h in probs]
    k_l = [qkv_ref[row_of(c), head_cols(1, h)].astype(F32) for c, h in probs]
    v_l = [qkv_ref[row_of(c), head_cols(2, h)].astype(F32) for c, h in probs]
    gcol_l = [Gs[c][:, h:h + 1] for c, h in probs]
    grow_l = [G_ts[c][h:h + 1, :] for c, h in probs]
    glast_l = [g[:, chunk - 1:chunk] for g in grow_l]
    bcol_l = [beta_all[row_of(c), h:h + 1] for c, h in probs]
    gamma_l = [jnp.exp(jnp.where(incl, gc - gr, -jnp.inf)) for gc, gr in zip(gcol_l, grow_l)]
    eg_l = [jnp.exp(g) for g in gcol_l]
    kb_l = [k * b for k, b in zip(k_l, bcol_l)]
    kkqk_l = [lax.dot_general(jnp.concatenate([kb, q], axis=0).astype(BF16), k.astype(BF16),
                              (((1,), (1,)), ((), ())), preferred_element_type=F32)
              for kb, q, k in zip(kb_l, q_l, k_l)]
    a_l = [jnp.where(strict, kk[:chunk] * gm, 0.0) for kk, gm in zip(kkqk_l, gamma_l)]
    qk_l = [kk[chunk:] * gm for kk, gm in zip(kkqk_l, gamma_l)]

    p_l = [jnp.where(level_masks[0], -a, 0.0) for a in a_l]
    t_l = [eye + x for x in p_l]
    for _ in range(int(math.log2(INV_BASE)) - 1):
        p_l = [mm(p, p) for p in p_l]
        t_l = [t + mm(p, t) for p, t in zip(p_l, t_l)]
    for lvl_mask in level_masks[1:]:
        tb_l = [t.astype(BF16) for t in t_l]
        ld_l = [jnp.dot(jnp.where(lvl_mask, a, 0.0).astype(BF16), t_b, preferred_element_type=F32)
                for a, t_b in zip(a_l, tb_l)]
        t_l = [t - jnp.dot(t_b, ld.astype(BF16), preferred_element_type=F32)
               for t, t_b, ld in zip(t_l, tb_l, ld_l)]
    sol_l = [mm(t, jnp.concatenate([v * b, kb * eg], axis=1))
             for t, v, b, kb, eg in zip(t_l, v_l, bcol_l, kb_l, eg_l)]
    wq_l = [jnp.concatenate([sol[:, DN_HEAD_DIM:], q * eg], axis=0).astype(BF16)
            for sol, q, eg in zip(sol_l, q_l, eg_l)]
    kdt_l = [(k * jnp.exp(gl - gc)).T.astype(BF16) for k, gl, gc in zip(k_l, glast_l, gcol_l)]

    states = [state_scr[h] for h in range(DN_HEADS)]
    for c in range(n_chunks):
        ids = [c * DN_HEADS + h for h in range(DN_HEADS)]
        ws_l = [jnp.dot(wq_l[i], states[h].astype(BF16), preferred_element_type=F32)
                for h, i in enumerate(ids)]
        vnew_l = [(sol_l[i][:, :DN_HEAD_DIM] - ws[:chunk]).astype(BF16) for i, ws in zip(ids, ws_l)]
        o_l = [ws[chunk:] + jnp.dot(qk_l[i].astype(BF16), vn, preferred_element_type=F32)
               for i, ws, vn in zip(ids, ws_l, vnew_l)]
        states = [states[h] * jnp.exp(glast_l[i]) + jnp.dot(kdt_l[i], vn, preferred_element_type=F32)
                  for (h, i), vn in zip(enumerate(ids), vnew_l)]
        for h, o in enumerate(o_l):
            cols = slice(h * DN_HEAD_DIM, (h + 1) * DN_HEAD_DIM)
            zf = z_ref[row_of(c), cols].astype(F32)
            o = o * _rms_scale(o) * onw_ref[...] * (zf * _sigmoid(zf))
            o_ref[row_of(c), cols] = o.astype(BF16)
    for h in range(DN_HEADS):
        state_scr[h] = states[h]


def _delta_mixer(proj, gates, a_log, dt_bias, out_norm_w):
    B, S, _ = proj.shape
    tb = min(DELTA_BLOCK, S)
    chunk = min(DELTA_CHUNK, tb)
    kern = functools.partial(_delta_kernel, tb=tb, chunk=chunk)
    return pl.pallas_call(
        kern,
        grid=(B, S // tb),
        in_specs=[
            pl.BlockSpec((None, tb, 3 * DN_WIDTH), lambda b, s: (b, s, 0)),
            pl.BlockSpec((None, tb, DN_WIDTH), lambda b, s: (b, s, 3)),
            pl.BlockSpec((None, tb, GATE_LANES), lambda b, s: (b, s, 0)),
            pl.BlockSpec((1, LANES), lambda b, s: (0, 0)),
            pl.BlockSpec((1, LANES), lambda b, s: (0, 0)),
            pl.BlockSpec((1, DN_HEAD_DIM), lambda b, s: (0, 0)),
        ],
        out_specs=pl.BlockSpec((None, tb, DN_WIDTH), lambda b, s: (b, s, 0)),
        out_shape=jax.ShapeDtypeStruct((B, S, DN_WIDTH), BF16),
        scratch_shapes=[pltpu.VMEM((DN_HEADS, DN_HEAD_DIM, DN_HEAD_DIM), F32)],
        compiler_params=_params(2),
        name="delta_mixer",
    )(proj, proj, gates, a_log, dt_bias, out_norm_w)


def _attn_kernel(q_ref, kp_ref, kc_ref, vp_ref, vc_ref, o_ref, lse_ref, *, bq):
    blk = ATTN_BLOCK
    q_i = lax.broadcasted_iota(jnp.int32, (blk, 2 * blk), 0)
    k_i = lax.broadcasted_iota(jnp.int32, (blk, 2 * blk), 1)
    band = (k_i >= q_i) & (k_i <= q_i + ATTN_SPAN)
    first_lo = jnp.where(pl.program_id(2) == 0, blk, 0)
    band_first = band & (k_i >= first_lo)
    lane = lax.broadcasted_iota(jnp.int32, (blk, LANES), 1)
    low_half = lane < SWA_HEAD_DIM

    col_of = lambda h: slice((h // 2) * LANES, (h // 2 + 1) * LANES)
    row_of = lambda s: slice(s * blk, (s + 1) * blk)

    def keys_values(s):
        if s == 0:
            return (jnp.concatenate([kp_ref[...], kc_ref[0:blk, :]], axis=0),
                    jnp.concatenate([vp_ref[...], vc_ref[0:blk, :]], axis=0), band_first)
        window = slice((s - 1) * blk, (s + 1) * blk)
        return kc_ref[window, :], vc_ref[window, :], band

    n_sub = bq // blk
    for s0 in range(0, n_sub, ATTN_GROUP):
        subs = range(s0, min(s0 + ATTN_GROUP, n_sub))
        kv = {s: keys_values(s) for s in subs}
        probs = [(s, h) for s in subs for h in range(SWA_HEADS)]
        qh_l = [jnp.where(low_half if h % 2 == 0 else jnp.logical_not(low_half),
                          q_ref[row_of(s), col_of(h)], jnp.zeros((blk, LANES), BF16))
                for s, h in probs]
        sc_l = [lax.dot_general(qh, kv[s][0][:, col_of(h)], (((1,), (1,)), ((), ())),
                                preferred_element_type=F32) for (s, h), qh in zip(probs, qh_l)]
        sc_l = [jnp.where(kv[s][2], sc, -jnp.inf) for (s, _), sc in zip(probs, sc_l)]
        m_l = [jnp.max(sc, axis=-1, keepdims=True) for sc in sc_l]
        p_l = [jnp.exp(sc - m) for sc, m in zip(sc_l, m_l)]
        l_l = [jnp.sum(p, axis=-1, keepdims=True) for p in p_l]
        pv_l = [jnp.dot(p.astype(BF16), kv[s][1][:, col_of(h)], preferred_element_type=F32)
                for (s, h), p in zip(probs, p_l)]
        o_l = [pv / l for pv, l in zip(pv_l, l_l)]
        for i, s in enumerate(subs):
            base = i * SWA_HEADS
            lse_acc = jnp.zeros((blk, LANES), F32)
            for h in range(SWA_HEADS):
                lse_acc = jnp.where(lane == h, m_l[base + h] + jnp.log(l_l[base + h]), lse_acc)
            lse_ref[row_of(s), :] = lse_acc
            for h in range(0, SWA_HEADS, 2):
                o_ref[row_of(s), col_of(h)] = jnp.where(low_half, o_l[base + h],
                                                        o_l[base + h + 1]).astype(BF16)


def _dilated_attention_pass(src, dil):
    blk = ATTN_BLOCK
    if dil == 1:
        B, L, _ = src.shape
        tile0 = ATTN_TILE0
        src = src.reshape(B, 1, L, src.shape[-1])
    else:
        B, _, L, _ = src.shape
        tile0 = 0
    bq = min(512, L)
    sub = bq // blk
    kern = functools.partial(_attn_kernel, bq=bq)

    def cur(col):
        return pl.BlockSpec((None, None, bq, PROJ_TILE), lambda b, r, i: (b, r, i, tile0 + col))

    def prev(col):
        return pl.BlockSpec((None, None, blk, PROJ_TILE),
                            lambda b, r, i: (b, r, jnp.maximum(i * sub - 1, 0), tile0 + col))

    return pl.pallas_call(
        kern,
        grid=(B, dil, L // bq),
        in_specs=[cur(0), prev(1), cur(1), prev(2), cur(2)],
        out_specs=[
            pl.BlockSpec((None, None, bq, SWA_WIDTH), lambda b, r, i: (b, r, i, 0)),
            pl.BlockSpec((None, None, bq, LANES), lambda b, r, i: (b, r, i, 0)),
        ],
        out_shape=[
            jax.ShapeDtypeStruct((B, dil, L, SWA_WIDTH), BF16),
            jax.ShapeDtypeStruct((B, dil, L, LANES), F32),
        ],
        compiler_params=_params(3),
        name=f"dilated_attn_d{dil}",
    )(src, src, src, src, src)


def _mix_residual(oa_ref, o_refs, l_refs, x, w_ref, e_ref, nw_ref, o_scr, l_scr, tm):
    n_planes = SWA_WIDTH // LANES
    tmp_slot = len(DILATIONS) - 1

    def token_order(o_ref, l_ref, level):
        if level == 0:
            return o_ref[0].astype(F32), l_ref[0]

        def from_input(r):
            o_res = o_ref[r].astype(F32)
            return [o_res[:, g * LANES:(g + 1) * LANES] for g in range(n_planes)], l_ref[r]

        get, rows = from_input, tm // DIL_STEP ** level
        for lvl in range(level, 0, -1):
            dst = level - 1 if lvl == 1 else tmp_slot
            for grp in range(DIL_STEP ** (lvl - 1)):
                for step in range(DIL_STEP):
                    planes, l_res = get(step * DIL_STEP ** (lvl - 1) + grp)
                    dst_rows = pl.ds(grp * rows * DIL_STEP + step, rows, stride=DIL_STEP)
                    for g in range(n_planes):
                        o_scr[dst * n_planes + g, dst_rows, :] = planes[g]
                    l_scr[dst, dst_rows, :] = l_res
            rows *= DIL_STEP

            def get(r, src=dst, n=rows):
                blk = slice(r * n, (r + 1) * n)
                return [o_scr[src * n_planes + g, blk, :] for g in range(n_planes)], l_scr[src, blk, :]

        planes, l_tok = get(0)
        return jnp.concatenate(planes, axis=1), l_tok

    pairs = [token_order(o, l, lvl) for lvl, (o, l) in enumerate(zip(o_refs, l_refs))]
    lses = [l for _, l in pairs]
    m = jnp.maximum(jnp.maximum(lses[0], lses[1]), lses[2])
    es = [jnp.exp(l - m) for l in lses]
    den = es[0] + es[1] + es[2]

    def expand(wt):
        hi = wt.astype(BF16)
        lo = (wt - hi.astype(F32)).astype(BF16)
        return jnp.dot(jnp.concatenate([hi, lo], axis=1), e_ref[...], preferred_element_type=F32)

    ob = None
    for e, (o, _) in zip(es, pairs):
        term = expand(e / den) * o
        ob = term if ob is None else ob + term
    y = (jnp.dot(oa_ref[...], w_ref[0:DN_WIDTH, :], preferred_element_type=F32)
         + jnp.dot(ob.astype(BF16), w_ref[DN_WIDTH:, :], preferred_element_type=F32))
    return x + y * _rms_scale(y) * nw_ref[...]


def _mix_ffn_kernel(oa_ref, o1_ref, o2_ref, o3_ref, l1_ref, l2_ref, l3_ref, x_ref, wo_ref, e_ref,
                    mnw_ref, nw_ref, wg_ref, wv_ref, cg_ref, cv_ref, bg_ref, bv_ref, wd_ref, pnw_ref,
                    out_ref, o_scr, l_scr, h_scr, u_scr, *, tm, tiles_per_seq):
    halo = BF16_ROWS

    @pl.when(pl.program_id(0) % tiles_per_seq == 0)
    def _():
        h_scr[0:halo, :] = jnp.zeros((halo, h_scr.shape[1]), BF16)

    x = _mix_residual(oa_ref, (o1_ref, o2_ref, o3_ref), (l1_ref, l2_ref, l3_ref), x_ref[...],
                      wo_ref, e_ref, mnw_ref, o_scr, l_scr, tm)
    h_scr[halo:, :] = (x * _rms_scale(x) * nw_ref[...]).astype(BF16)

    def conv(w_ref, c_ref, b_ref):
        u_scr[...] = jnp.dot(h_scr[...], w_ref[...], preferred_element_type=F32)
        acc = u_scr[halo:halo + tm, :] * c_ref[FFN_CONV - 1:FFN_CONV, :]
        for t in range(FFN_CONV - 1):
            off = halo - (FFN_CONV - 1) + t
            acc = acc + u_scr[off:off + tm, :] * c_ref[t:t + 1, :]
        return acc + b_ref[...]

    gate = conv(wg_ref, cg_ref, bg_ref)
    cdf = 0.5 * (1.0 + jnp.tanh(math.sqrt(2.0 / math.pi) * (gate + 0.044715 * (gate * gate * gate))))
    act = (gate * cdf) * conv(wv_ref, cv_ref, bv_ref)
    y = jnp.dot(act.astype(BF16), wd_ref[...], preferred_element_type=F32)
    out_ref[...] = x + y * _rms_scale(y) * pnw_ref[...]
    h_scr[0:halo, :] = h_scr[tm:tm + halo, :]


def _mix_ffn(o_a, outs, lses, x, w_out, expand_mat, mix_norm_w, pre_w, w_up, conv_w, conv_b, w_down,
             post_w):
    B, S, D = x.shape
    d_ff = w_down.shape[0]
    tm = min(TOKEN_TILE, S)
    tps = S // tm

    def tok(width):
        return pl.BlockSpec((None, tm, width), lambda i: (i // tps, i % tps, 0))

    def grouped(width):
        return [pl.BlockSpec((None, d, tm // d, width), lambda i: (i // tps, 0, i % tps, 0))
                for d in DILATIONS]

    def const(shape, col=0):
        return pl.BlockSpec(shape, lambda i: (0, col))

    n_slots = len(DILATIONS)
    return pl.pallas_call(
        functools.partial(_mix_ffn_kernel, tm=tm, tiles_per_seq=tps),
        grid=(B * tps,),
        in_specs=[tok(DN_WIDTH)] + grouped(SWA_WIDTH) + grouped(LANES) + [
            tok(D),
            const((DN_WIDTH + SWA_WIDTH, D)),
            const((2 * LANES, SWA_WIDTH)),
            const((1, D)),
            const((1, D)),
            const((D, d_ff)), const((D, d_ff), 1),
            const((FFN_CONV, d_ff)), const((FFN_CONV, d_ff), 1),
            const((1, d_ff)), const((1, d_ff), 1),
            const((d_ff, D)),
            const((1, D))],
        out_specs=tok(D),
        out_shape=jax.ShapeDtypeStruct((B, S, D), F32),
        scratch_shapes=[pltpu.VMEM((n_slots * SWA_WIDTH // LANES, tm, LANES), F32),
                        pltpu.VMEM((n_slots, tm, LANES), F32),
                        pltpu.VMEM((tm + BF16_ROWS, D), BF16),
                        pltpu.VMEM((tm + BF16_ROWS, d_ff), F32)],
        compiler_params=_params(1),
        name="mix_ffn",
    )(o_a, *outs, *lses, x, w_out, expand_mat, mix_norm_w, pre_w, w_up, w_up, conv_w, conv_w,
      conv_b, conv_b, w_down, post_w)


def _rope_tables(S):
    pos = jnp.arange(S, dtype=F32)
    inv_freq = ROPE_THETA ** (-jnp.arange(0, ROPE_DIM, 2, dtype=F32) / ROPE_DIM)
    ang = pos[:, None] * inv_freq[None, :]
    cos, sin = jnp.cos(ang), jnp.sin(ang)
    pad = SWA_HEAD_DIM - ROPE_DIM
    ones = jnp.ones((S, pad), F32)
    zeros_h = jnp.zeros((S, ROPE_HALF), F32)
    zeros_p = jnp.zeros((S, pad), F32)
    cos_head = jnp.concatenate([cos, cos, ones], axis=1)
    sa_head = jnp.concatenate([-sin, zeros_h, zeros_p], axis=1)
    sb_head = jnp.concatenate([zeros_h, sin, zeros_p], axis=1)
    tile = lambda t: jnp.tile(t, (1, SWA_HEADS))
    return tile(cos_head), tile(sa_head), tile(sb_head)


def _pad_lanes(v):
    return jnp.zeros((1, LANES), F32).at[0, :v.shape[0]].set(v.astype(F32))


def kernel(x, pre_mix_norm, w_in, dn_conv, dn_a_log, dn_dt_bias, dn_out_norm, w_out, post_mix_norm,
           pre_ffn_norm, ffn_up, ffn_conv, ffn_conv_bias, ffn_down, post_ffn_norm):
    B, S, D = x.shape
    depth = w_in.shape[0]
    assert S % (max(DILATIONS) * ATTN_BLOCK) == 0 and S % TOKEN_TILE == 0
    cos_f, sin_a, sin_b = _rope_tables(S)
    qkvz = 4 * DN_WIDTH
    n_gate = 2 * DN_HEADS
    head_of_lane = jnp.arange(SWA_WIDTH) // SWA_HEAD_DIM
    expand_mat = (jnp.arange(2 * LANES)[:, None] % LANES == head_of_lane[None, :]).astype(BF16)
    row = lambda v: v.reshape(1, -1).astype(F32)

    for l in range(depth):
        w_l = w_in[l]
        w_main = jnp.concatenate([w_l[:, :qkvz], w_l[:, qkvz + n_gate:]], axis=1).astype(BF16)
        w_gate = jnp.zeros((D, GATE_LANES), F32)
        w_gate = w_gate.at[:, :DN_HEADS].set(w_l[:, qkvz:qkvz + DN_HEADS])
        w_gate = w_gate.at[:, LANES:LANES + DN_HEADS].set(w_l[:, qkvz + DN_HEADS:qkvz + n_gate])
        proj, gates, *grouped = _in_proj(x, row(pre_mix_norm[l]), w_main, w_gate.astype(BF16),
                                         dn_conv[l].astype(F32), cos_f, sin_a, sin_b)
        o_a = _delta_mixer(proj, gates, _pad_lanes(dn_a_log[l]), _pad_lanes(dn_dt_bias[l]),
                           row(dn_out_norm[l]))
        outs, lses = zip(*[_dilated_attention_pass(src, d)
                           for src, d in zip([proj] + grouped, DILATIONS)])
        x = _mix_ffn(o_a, outs, lses, x, w_out[l].astype(BF16), expand_mat, row(post_mix_norm[l]),
                     row(pre_ffn_norm[l]), ffn_up[l].astype(BF16), ffn_conv[l].astype(F32),
                     row(ffn_conv_bias[l]), ffn_down[l].astype(BF16), row(post_ffn_norm[l]))
    return x
```

```python
import functools
import math

import jax
import jax.numpy as jnp
from jax import lax
from jax.experimental import pallas as pl
from jax.experimental.pallas import tpu as pltpu

F32 = jnp.float32
BF16 = jnp.bfloat16

DN_HEADS = 4
DN_HEAD_DIM = 128
DN_WIDTH = DN_HEADS * DN_HEAD_DIM
DN_CONV = 4
SWA_HEADS = 8
SWA_HEAD_DIM = 64
SWA_WIDTH = SWA_HEADS * SWA_HEAD_DIM
DILATIONS = (1, 4, 16)
DIL_STEP = 4
ATTN_BLOCK = 128
ATTN_SPAN = 128
ROPE_THETA = 500000.0
ROPE_DIM = SWA_HEAD_DIM // 4
ROPE_HALF = ROPE_DIM // 2
FFN_CONV = 3
NORM_EPS = 1e-6

LANES = 128
SUBLANES = 8
BF16_ROWS = 16
VMEM_LIMIT_BYTES = 56 * 1024 * 1024

TOKEN_TILE = 512
DELTA_BLOCK = 512
DELTA_CHUNK = 128
INV_BASE = 8
PROJ_TILE = 512
N_PROJ_TILES = 7
ATTN_TILE0 = 4
GATE_LANES = 2 * LANES
ATTN_ROWS_PER_STEP = 1024


def _params(n_axes):
    return pltpu.CompilerParams(
        dimension_semantics=("arbitrary",) * n_axes, vmem_limit_bytes=VMEM_LIMIT_BYTES)


def _sigmoid(x):
    return 1.0 / (1.0 + jnp.exp(-x))


def _rms_scale(x):
    return lax.rsqrt(jnp.mean(x * x, axis=-1, keepdims=True) + NORM_EPS)


def _split3(x):
    hi = x.astype(BF16)
    r = x - hi.astype(F32)
    mid = r.astype(BF16)
    lo = (r - mid.astype(F32)).astype(BF16)
    return hi, mid, lo


def _in_proj_kernel(x_ref, xh_ref, nw_ref, w_ref, wg_ref, cw_ref, cos_ref, sa_ref, sb_ref,
                    proj_ref, gate_ref, *rest, tm):
    res_refs, (h_scr, conv_scr, att_scr, mid_scr) = rest[:-4], rest[-4:]
    groups_per_tile = PROJ_TILE // LANES
    halo = BF16_ROWS
    x = x_ref[...]
    h_scr[halo:, :] = (x * _rms_scale(x) * nw_ref[...]).astype(BF16)
    xh = xh_ref[...]
    h_scr[0:halo, :] = jnp.where(pl.program_id(0) == 0, 0.0,
                                 xh * _rms_scale(xh) * nw_ref[...]).astype(BF16)
    gate_ref[...] = jnp.dot(h_scr[halo:, :], wg_ref[...], preferred_element_type=F32)

    def conv_silu_norm(j, cols):
        conv_scr[j] = jnp.dot(h_scr[...], w_ref[:, cols], preferred_element_type=F32)
        acc = conv_scr[j, halo:halo + tm, :] * cw_ref[DN_CONV - 1:DN_CONV, cols]
        for t in range(DN_CONV - 1):
            off = halo - (DN_CONV - 1) + t
            acc = acc + conv_scr[j, off:off + tm, :] * cw_ref[t:t + 1, cols]
        y = acc * _sigmoid(acc)
        if j == 2:
            return y
        scale = DN_HEAD_DIM ** -0.5 if j == 0 else 1.0
        heads = []
        for hd in range(DN_HEADS):
            t = y[:, hd * DN_HEAD_DIM:(hd + 1) * DN_HEAD_DIM]
            heads.append(t * (lax.rsqrt(jnp.sum(t * t, axis=-1, keepdims=True) + NORM_EPS) * scale))
        return jnp.concatenate(heads, axis=1)

    def regroup(planes):
        src_scr, rows_per_group = att_scr, tm
        for level, res_ref in enumerate(res_refs):
            n_out = rows_per_group // DIL_STEP
            for grp in range(DIL_STEP ** level):
                for step in range(DIL_STEP):
                    r = step * DIL_STEP ** level + grp
                    start = grp * rows_per_group + step
                    for g in planes:
                        t = src_scr[g, pl.ds(start, n_out, stride=DIL_STEP), :]
                        res_ref[r, :, g * LANES:(g + 1) * LANES] = t.astype(BF16)
                        if level + 1 < len(res_refs):
                            mid_scr[g, r * n_out:(r + 1) * n_out, :] = t
            src_scr, rows_per_group = mid_scr, n_out

    tile_order = list(range(ATTN_TILE0, N_PROJ_TILES)) + list(range(ATTN_TILE0))
    for j in tile_order:
        if j < N_PROJ_TILES - ATTN_TILE0:
            regroup(range(j * groups_per_tile, (j + 1) * groups_per_tile))
        cols = slice(j * PROJ_TILE, (j + 1) * PROJ_TILE)
        if j < 3:
            proj_ref[:, cols] = conv_silu_norm(j, cols).astype(BF16)
            continue
        y = jnp.dot(h_scr[halo:, :], w_ref[:, cols], preferred_element_type=F32)
        if j in (4, 5):
            if j == 4:
                y = y * (SWA_HEAD_DIM ** -0.5)
            y = (y * cos_ref[...]
                 + pltpu.roll(y, PROJ_TILE - ROPE_HALF, 1) * sa_ref[...]
                 + pltpu.roll(y, ROPE_HALF, 1) * sb_ref[...])
        proj_ref[:, cols] = y.astype(BF16)
        if j >= ATTN_TILE0:
            for g in range(groups_per_tile):
                att_scr[(j - ATTN_TILE0) * groups_per_tile + g] = y[:, g * LANES:(g + 1) * LANES]


def _in_proj(x, norm_w, w_main, w_gate, conv_w, cos_f, sin_a, sin_b):
    B, S, D = x.shape
    tm = min(TOKEN_TILE, S)
    halo = BF16_ROWS
    hb = tm // halo
    n_cols = N_PROJ_TILES * PROJ_TILE
    att_cols = 3 * SWA_WIDTH
    res_specs = [pl.BlockSpec((None, d, tm // d, att_cols), lambda s, b: (b, 0, s, 0))
                 for d in DILATIONS[1:]]
    res_shapes = [jax.ShapeDtypeStruct((B, d, S // d, att_cols), BF16) for d in DILATIONS[1:]]
    return pl.pallas_call(
        functools.partial(_in_proj_kernel, tm=tm),
        grid=(S // tm, B),
        in_specs=[
            pl.BlockSpec((None, tm, D), lambda s, b: (b, s, 0)),
            pl.BlockSpec((None, halo, D), lambda s, b: (b, jnp.maximum(s * hb - 1, 0), 0)),
            pl.BlockSpec((1, D), lambda s, b: (0, 0)),
            pl.BlockSpec((D, n_cols), lambda s, b: (0, 0)),
            pl.BlockSpec((D, GATE_LANES), lambda s, b: (0, 0)),
            pl.BlockSpec((DN_CONV, 3 * DN_WIDTH), lambda s, b: (0, 0)),
            pl.BlockSpec((tm, PROJ_TILE), lambda s, b: (s, 0)),
            pl.BlockSpec((tm, PROJ_TILE), lambda s, b: (s, 0)),
            pl.BlockSpec((tm, PROJ_TILE), lambda s, b: (s, 0)),
        ],
        out_specs=[
            pl.BlockSpec((None, tm, n_cols), lambda s, b: (b, s, 0)),
            pl.BlockSpec((None, tm, GATE_LANES), lambda s, b: (b, s, 0)),
        ] + res_specs,
        out_shape=[
            jax.ShapeDtypeStruct((B, S, n_cols), BF16),
            jax.ShapeDtypeStruct((B, S, GATE_LANES), F32),
        ] + res_shapes,
        scratch_shapes=[pltpu.VMEM((tm + halo, D), BF16), pltpu.VMEM((3, tm + halo, PROJ_TILE), F32)]
        + [pltpu.VMEM((att_cols // LANES, tm, LANES), F32)] * 2,
        compiler_params=_params(2),
        name="in_proj",
    )(x, x, norm_w, w_main, w_gate, conv_w, cos_f, sin_a, sin_b)


def _delta_kernel(qkv_ref, z_ref, gate_ref, alog_ref, dtb_ref, onw_ref, o_ref, state_scr,
                  *, tb, chunk):
    @pl.when(pl.program_id(1) == 0)
    def _():
        state_scr[...] = jnp.zeros_like(state_scr)

    gates = gate_ref[...]
    beta_all = _sigmoid(gates[:, :LANES])
    a_in = gates[:, LANES:] + dtb_ref[...]
    softplus = jnp.maximum(a_in, 0.0) + jnp.log(1.0 + jnp.exp(-jnp.abs(a_in)))
    g_all = -jnp.exp(alog_ref[...]) * softplus
    g_parts = _split3(g_all)

    r_i = lax.broadcasted_iota(jnp.int32, (chunk, chunk), 0)
    c_i = lax.broadcasted_iota(jnp.int32, (chunk, chunk), 1)
    incl = r_i >= c_i
    strict = r_i > c_i
    tri = jnp.where(incl, 1.0, 0.0).astype(BF16)
    eye = jnp.where(r_i == c_i, 1.0, 0.0).astype(F32)

    def same_block(log_size):
        return (r_i >> log_size) == (c_i >> log_size)

    log_s = int(math.log2(INV_BASE))
    level_masks = [same_block(log_s)]
    while (1 << log_s) < chunk:
        level_masks.append(same_block(log_s + 1) & (((r_i >> log_s) & 1) == 1)
                           & (((c_i >> log_s) & 1) == 0))
        log_s += 1

    def head_cols(part, h):
        return slice(part * DN_WIDTH + h * DN_HEAD_DIM, part * DN_WIDTH + (h + 1) * DN_HEAD_DIM)

    def mm(a, b):
        return jnp.dot(a.astype(BF16), b.astype(BF16), preferred_element_type=F32)

    n_chunks = tb // chunk
    probs = [(c, h) for c in range(n_chunks) for h in range(DN_HEADS)]
    row_of = lambda c: slice(c * chunk, (c + 1) * chunk)

    Gs = [sum(jnp.dot(tri, p[row_of(c)], preferred_element_type=F32) for p in g_parts)
          for c in range(n_chunks)]
    G_ts = [G.T for G in Gs]

    q_l = [qkv_ref[row_of(c), head_cols(0, h)].astype(F32) for c, h in probs]
    k_l = [qkv_ref[row_of(c), head_cols(1, h)].astype(F32) for c, h in probs]
    v_l = [qkv_ref[row_of(c), head_cols(2, h)].astype(F32) for c, h in probs]
    gcol_l = [Gs[c][:, h:h + 1] for c, h in probs]
    grow_l = [G_ts[c][h:h + 1, :] for c, h in probs]
    glast_l = [g[:, chunk - 1:chunk] for g in grow_l]
    bcol_l = [beta_all[row_of(c), h:h + 1] for c, h in probs]
    gamma_l = [jnp.exp(jnp.where(incl, gc - gr, -jnp.inf)) for gc, gr in zip(gcol_l, grow_l)]
    eg_l = [jnp.exp(g) for g in gcol_l]
    kb_l = [k * b for k, b in zip(k_l, bcol_l)]
    kkqk_l = [lax.dot_general(jnp.concatenate([kb, q], axis=0).astype(BF16), k.astype(BF16),
                              (((1,), (1,)), ((), ())), preferred_element_type=F32)
              for kb, q, k in zip(kb_l, q_l, k_l)]
    a_l = [jnp.where(strict, kk[:chunk] * gm, 0.0) for kk, gm in zip(kkqk_l, gamma_l)]
    qk_l = [kk[chunk:] * gm for kk, gm in zip(kkqk_l, gamma_l)]

    p_l = [jnp.where(level_masks[0], -a, 0.0) for a in a_l]
    t_l = [eye + x for x in p_l]
    for _ in range(int(math.log2(INV_BASE)) - 1):
        p_l = [mm(p, p) for p in p_l]
        t_l = [t + mm(p, t) for p, t in zip(p_l, t_l)]
    for lvl_mask in level_masks[1:]:
        tb_l = [t.astype(BF16) for t in t_l]
        ld_l = [jnp.dot(jnp.where(lvl_mask, a, 0.0).astype(BF16), t_b, preferred_element_type=F32)
                for a, t_b in zip(a_l, tb_l)]
        t_l = [t - jnp.dot(t_b, ld.astype(BF16), preferred_element_type=F32)
               for t, t_b, ld in zip(t_l, tb_l, ld_l)]
    sol_l = [mm(t, jnp.concatenate([v * b, kb * eg], axis=1))
             for t, v, b, kb, eg in zip(t_l, v_l, bcol_l, kb_l, eg_l)]
    wq_l = [jnp.concatenate([sol[:, DN_HEAD_DIM:], q * eg], axis=0).astype(BF16)
            for sol, q, eg in zip(sol_l, q_l, eg_l)]
    kdt_l = [(k * jnp.exp(gl - gc)).T.astype(BF16) for k, gl, gc in zip(k_l, glast_l, gcol_l)]

    states = [state_scr[h] for h in range(DN_HEADS)]
    for c in range(n_chunks):
        ids = [c * DN_HEADS + h for h in range(DN_HEADS)]
        ws_l = [jnp.dot(wq_l[i], states[h].astype(BF16), preferred_element_type=F32)
                for h, i in enumerate(ids)]
        vnew_l = [(sol_l[i][:, :DN_HEAD_DIM] - ws[:chunk]).astype(BF16) for i, ws in zip(ids, ws_l)]
        o_l = [ws[chunk:] + jnp.dot(qk_l[i].astype(BF16), vn, preferred_element_type=F32)
               for i, ws, vn in zip(ids, ws_l, vnew_l)]
        states = [states[h] * jnp.exp(glast_l[i]) + jnp.dot(kdt_l[i], vn, preferred_element_type=F32)
                  for (h, i), vn in zip(enumerate(ids), vnew_l)]
        for h, o in enumerate(o_l):
            cols = slice(h * DN_HEAD_DIM, (h + 1) * DN_HEAD_DIM)
            zf = z_ref[row_of(c), cols].astype(F32)
            o = o * _rms_scale(o) * onw_ref[...] * (zf * _sigmoid(zf))
            o_ref[row_of(c), cols] = o.astype(BF16)
    for h in range(DN_HEADS):
        state_scr[h] = states[h]


def _delta_mixer(proj, gates, a_log, dt_bias, out_norm_w):
    B, S, _ = proj.shape
    tb = min(DELTA_BLOCK, S)
    chunk = min(DELTA_CHUNK, tb)
    kern = functools.partial(_delta_kernel, tb=tb, chunk=chunk)
    return pl.pallas_call(
        kern,
        grid=(B, S // tb),
        in_specs=[
            pl.BlockSpec((None, tb, 3 * DN_WIDTH), lambda b, s: (b, s, 0)),
            pl.BlockSpec((None, tb, DN_WIDTH), lambda b, s: (b, s, 3)),
            pl.BlockSpec((None, tb, GATE_LANES), lambda b, s: (b, s, 0)),
            pl.BlockSpec((1, LANES), lambda b, s: (0, 0)),
            pl.BlockSpec((1, LANES), lambda b, s: (0, 0)),
            pl.BlockSpec((1, DN_HEAD_DIM), lambda b, s: (0, 0)),
        ],
        out_specs=pl.BlockSpec((None, tb, DN_WIDTH), lambda b, s: (b, s, 0)),
        out_shape=jax.ShapeDtypeStruct((B, S, DN_WIDTH), BF16),
        scratch_shapes=[pltpu.VMEM((DN_HEADS, DN_HEAD_DIM, DN_HEAD_DIM), F32)],
        compiler_params=_params(2),
        name="delta_mixer",
    )(proj, proj, gates, a_log, dt_bias, out_norm_w)


def _attn_kernel(q_ref, kp_ref, kc_ref, vp_ref, vc_ref, o_ref, lse_ref, *, bq, n_res):
    blk = ATTN_BLOCK
    q_i = lax.broadcasted_iota(jnp.int32, (blk, 2 * blk), 0)
    k_i = lax.broadcasted_iota(jnp.int32, (blk, 2 * blk), 1)
    band = (k_i >= q_i) & (k_i <= q_i + ATTN_SPAN)
    first_lo = jnp.where(pl.program_id(2) == 0, blk, 0)
    band_first = band & (k_i >= first_lo)
    lane = lax.broadcasted_iota(jnp.int32, (blk, LANES), 1)
    low_half = lane < SWA_HEAD_DIM

    col_of = lambda h: slice((h // 2) * LANES, (h // 2 + 1) * LANES)
    row_of = lambda s: slice(s * blk, (s + 1) * blk)

    for r in range(n_res):
        for s in range(bq // blk):
            if s == 0:
                k = jnp.concatenate([kp_ref[r], kc_ref[r, 0:blk, :]], axis=0)
                v = jnp.concatenate([vp_ref[r], vc_ref[r, 0:blk, :]], axis=0)
                valid = band_first
            else:
                window = slice((s - 1) * blk, (s + 1) * blk)
                k, v, valid = kc_ref[r, window, :], vc_ref[r, window, :], band
            heads = range(SWA_HEADS)
            qh_l = [jnp.where(low_half if h % 2 == 0 else jnp.logical_not(low_half),
                              q_ref[r, row_of(s), col_of(h)], jnp.zeros((blk, LANES), BF16))
                    for h in heads]
            sc_l = [lax.dot_general(qh, k[:, col_of(h)], (((1,), (1,)), ((), ())),
                                    preferred_element_type=F32) for h, qh in zip(heads, qh_l)]
            sc_l = [jnp.where(valid, sc, -jnp.inf) for sc in sc_l]
            m_l = [jnp.max(sc, axis=-1, keepdims=True) for sc in sc_l]
            p_l = [jnp.exp(sc - m) for sc, m in zip(sc_l, m_l)]
            l_l = [jnp.sum(p, axis=-1, keepdims=True) for p in p_l]
            pv_l = [jnp.dot(p.astype(BF16), v[:, col_of(h)], preferred_element_type=F32)
                    for h, p in zip(heads, p_l)]
            o_l = [pv / l for pv, l in zip(pv_l, l_l)]
            lse_acc = jnp.zeros((blk, LANES), F32)
            for h in heads:
                lse_acc = jnp.where(lane == h, m_l[h] + jnp.log(l_l[h]), lse_acc)
            lse_ref[r, row_of(s), :] = lse_acc
            for h in range(0, SWA_HEADS, 2):
                o_ref[r, row_of(s), col_of(h)] = jnp.where(low_half, o_l[h], o_l[h + 1]).astype(BF16)


def _dilated_attention_pass(src, dil):
    blk = ATTN_BLOCK
    if dil == 1:
        B, L, _ = src.shape
        tile0 = ATTN_TILE0
        src = src.reshape(B, 1, L, src.shape[-1])
    else:
        B, _, L, _ = src.shape
        tile0 = 0
    bq = min(ATTN_ROWS_PER_STEP, L)
    n_res = min(dil, ATTN_ROWS_PER_STEP // bq)
    sub = bq // blk
    kern = functools.partial(_attn_kernel, bq=bq, n_res=n_res)

    def cur(col):
        return pl.BlockSpec((None, n_res, bq, PROJ_TILE), lambda b, r, i: (b, r, i, tile0 + col))

    def prev(col):
        return pl.BlockSpec((None, n_res, blk, PROJ_TILE),
                            lambda b, r, i: (b, r, jnp.maximum(i * sub - 1, 0), tile0 + col))

    return pl.pallas_call(
        kern,
        grid=(B, dil // n_res, L // bq),
        in_specs=[cur(0), prev(1), cur(1), prev(2), cur(2)],
        out_specs=[
            pl.BlockSpec((None, n_res, bq, SWA_WIDTH), lambda b, r, i: (b, r, i, 0)),
            pl.BlockSpec((None, n_res, bq, LANES), lambda b, r, i: (b, r, i, 0)),
        ],
        out_shape=[
            jax.ShapeDtypeStruct((B, dil, L, SWA_WIDTH), BF16),
            jax.ShapeDtypeStruct((B, dil, L, LANES), F32),
        ],
        compiler_params=_params(3),
        name=f"dilated_attn_d{dil}",
    )(src, src, src, src, src)


def _mix_residual(oa_ref, o_refs, l_refs, x, w_ref, e_ref, nw_ref, o_scr, l_scr, tm):
    n_planes = SWA_WIDTH // LANES
    tmp_slot = len(DILATIONS) - 1

    def token_order(o_ref, l_ref, level):
        if level == 0:
            return o_ref[0].astype(F32), l_ref[0]

        def from_input(r):
            o_res = o_ref[r].astype(F32)
            return [o_res[:, g * LANES:(g + 1) * LANES] for g in range(n_planes)], l_ref[r]

        get, rows = from_input, tm // DIL_STEP ** level
        for lvl in range(level, 0, -1):
            dst = level - 1 if lvl == 1 else tmp_slot
            for grp in range(DIL_STEP ** (lvl - 1)):
                for step in range(DIL_STEP):
                    planes, l_res = get(step * DIL_STEP ** (lvl - 1) + grp)
                    dst_rows = pl.ds(grp * rows * DIL_STEP + step, rows, stride=DIL_STEP)
                    for g in range(n_planes):
                        o_scr[dst * n_planes + g, dst_rows, :] = planes[g]
                    l_scr[dst, dst_rows, :] = l_res
            rows *= DIL_STEP

            def get(r, src=dst, n=rows):
                blk = slice(r * n, (r + 1) * n)
                return [o_scr[src * n_planes + g, blk, :] for g in range(n_planes)], l_scr[src, blk, :]

        planes, l_tok = get(0)
        return jnp.concatenate(planes, axis=1), l_tok

    pairs = [token_order(o, l, lvl) for lvl, (o, l) in enumerate(zip(o_refs, l_refs))]
    lses = [l for _, l in pairs]
    m = jnp.maximum(jnp.maximum(lses[0], lses[1]), lses[2])
    es = [jnp.exp(l - m) for l in lses]
    den = es[0] + es[1] + es[2]

    def expand(wt):
        hi = wt.astype(BF16)
        lo = (wt - hi.astype(F32)).astype(BF16)
        return jnp.dot(jnp.concatenate([hi, lo], axis=1), e_ref[...], preferred_element_type=F32)

    ob = None
    for e, (o, _) in zip(es, pairs):
        term = expand(e / den) * o
        ob = term if ob is None else ob + term
    y = (jnp.dot(oa_ref[...], w_ref[0:DN_WIDTH, :], preferred_element_type=F32)
         + jnp.dot(ob.astype(BF16), w_ref[DN_WIDTH:, :], preferred_element_type=F32))
    return x + y * _rms_scale(y) * nw_ref[...]


def _mix_ffn_kernel(oa_ref, o1_ref, o2_ref, o3_ref, l1_ref, l2_ref, l3_ref, x_ref, wo_ref, e_ref,
                    mnw_ref, nw_ref, wg_ref, wv_ref, cg_ref, cv_ref, bg_ref, bv_ref, wd_ref, pnw_ref,
                    out_ref, o_scr, l_scr, h_scr, u_scr, *, tm, tiles_per_seq):
    halo = BF16_ROWS

    @pl.when(pl.program_id(0) % tiles_per_seq == 0)
    def _():
        h_scr[0:halo, :] = jnp.zeros((halo, h_scr.shape[1]), BF16)

    x = _mix_residual(oa_ref, (o1_ref, o2_ref, o3_ref), (l1_ref, l2_ref, l3_ref), x_ref[...],
                      wo_ref, e_ref, mnw_ref, o_scr, l_scr, tm)
    h_scr[halo:, :] = (x * _rms_scale(x) * nw_ref[...]).astype(BF16)

    def conv(w_ref, c_ref, b_ref):
        u_scr[...] = jnp.dot(h_scr[...], w_ref[...], preferred_element_type=F32)
        acc = u_scr[halo:halo + tm, :] * c_ref[FFN_CONV - 1:FFN_CONV, :]
        for t in range(FFN_CONV - 1):
            off = halo - (FFN_CONV - 1) + t
            acc = acc + u_scr[off:off + tm, :] * c_ref[t:t + 1, :]
        return acc + b_ref[...]

    gate = conv(wg_ref, cg_ref, bg_ref)
    c = math.sqrt(2.0 / math.pi)
    inner = gate * (c + (0.044715 * c) * (gate * gate))
    act = (gate * (1.0 + jnp.tanh(inner))) * conv(wv_ref, cv_ref, bv_ref)
    y = jnp.dot(act.astype(BF16), wd_ref[...], preferred_element_type=F32)
    out_ref[...] = x + y * _rms_scale(y) * pnw_ref[...]
    h_scr[0:halo, :] = h_scr[tm:tm + halo, :]


def _mix_ffn(o_a, outs, lses, x, w_out, expand_mat, mix_norm_w, pre_w, w_up, conv_w, conv_b, w_down,
             post_w):
    B, S, D = x.shape
    d_ff = w_down.shape[0]
    tm = min(TOKEN_TILE, S)
    tps = S // tm

    def tok(width):
        return pl.BlockSpec((None, tm, width), lambda i: (i // tps, i % tps, 0))

    def grouped(width):
        return [pl.BlockSpec((None, d, tm // d, width), lambda i: (i // tps, 0, i % tps, 0))
                for d in DILATIONS]

    def const(shape, col=0):
        return pl.BlockSpec(shape, lambda i: (0, col))

    n_slots = len(DILATIONS)
    return pl.pallas_call(
        functools.partial(_mix_ffn_kernel, tm=tm, tiles_per_seq=tps),
        grid=(B * tps,),
        in_specs=[tok(DN_WIDTH)] + grouped(SWA_WIDTH) + grouped(LANES) + [
            tok(D),
            const((DN_WIDTH + SWA_WIDTH, D)),
            const((2 * LANES, SWA_WIDTH)),
            const((1, D)),
            const((1, D)),
            const((D, d_ff)), const((D, d_ff), 1),
            const((FFN_CONV, d_ff)), const((FFN_CONV, d_ff), 1),
            const((1, d_ff)), const((1, d_ff), 1),
            const((d_ff, D)),
            const((1, D))],
        out_specs=tok(D),
        out_shape=jax.ShapeDtypeStruct((B, S, D), F32),
        scratch_shapes=[pltpu.VMEM((n_slots * SWA_WIDTH // LANES, tm, LANES), F32),
                        pltpu.VMEM((n_slots, tm, LANES), F32),
                        pltpu.VMEM((tm + BF16_ROWS, D), BF16),
                        pltpu.VMEM((tm + BF16_ROWS, d_ff), F32)],
        compiler_params=_params(1),
        name="mix_ffn",
    )(o_a, *outs, *lses, x, w_out, expand_mat, mix_norm_w, pre_w, w_up, w_up, conv_w, conv_w,
      conv_b, conv_b, w_down, post_w)


def _rope_tables(S):
    pos = jnp.arange(S, dtype=F32)
    inv_freq = ROPE_THETA ** (-jnp.arange(0, ROPE_DIM, 2, dtype=F32) / ROPE_DIM)
    ang = pos[:, None] * inv_freq[None, :]
    cos, sin = jnp.cos(ang), jnp.sin(ang)
    pad = SWA_HEAD_DIM - ROPE_DIM
    ones = jnp.ones((S, pad), F32)
    zeros_h = jnp.zeros((S, ROPE_HALF), F32)
    zeros_p = jnp.zeros((S, pad), F32)
    cos_head = jnp.concatenate([cos, cos, ones], axis=1)
    sa_head = jnp.concatenate([-sin, zeros_h, zeros_p], axis=1)
    sb_head = jnp.concatenate([zeros_h, sin, zeros_p], axis=1)
    tile = lambda t: jnp.tile(t, (1, SWA_HEADS))
    return tile(cos_head), tile(sa_head), tile(sb_head)


def _pad_lanes(v):
    return jnp.zeros((1, LANES), F32).at[0, :v.shape[0]].set(v.astype(F32))


def kernel(x, pre_mix_norm, w_in, dn_conv, dn_a_log, dn_dt_bias, dn_out_norm, w_out, post_mix_norm,
           pre_ffn_norm, ffn_up, ffn_conv, ffn_conv_bias, ffn_down, post_ffn_norm):
    B, S, D = x.shape
    depth = w_in.shape[0]
    assert S % (max(DILATIONS) * ATTN_BLOCK) == 0 and S % TOKEN_TILE == 0
    cos_f, sin_a, sin_b = _rope_tables(S)
    qkvz = 4 * DN_WIDTH
    n_gate = 2 * DN_HEADS
    head_of_lane = jnp.arange(SWA_WIDTH) // SWA_HEAD_DIM
    expand_mat = (jnp.arange(2 * LANES)[:, None] % LANES == head_of_lane[None, :]).astype(BF16)
    row = lambda v: v.reshape(1, -1).astype(F32)
    d_ff = ffn_down.shape[1]
    val_half = jnp.where(jnp.arange(2 * d_ff) < d_ff, 1.0, 0.5).astype(F32)[None, :]

    for l in range(depth):
        w_l = w_in[l]
        w_main = jnp.concatenate([w_l[:, :qkvz], w_l[:, qkvz + n_gate:]], axis=1).astype(BF16)
        w_gate = jnp.zeros((D, GATE_LANES), F32)
        w_gate = w_gate.at[:, :DN_HEADS].set(w_l[:, qkvz:qkvz + DN_HEADS])
        w_gate = w_gate.at[:, LANES:LANES + DN_HEADS].set(w_l[:, qkvz + DN_HEADS:qkvz + n_gate])
        proj, gates, *grouped = _in_proj(x, row(pre_mix_norm[l]), w_main, w_gate.astype(BF16),
                                         dn_conv[l].astype(F32), cos_f, sin_a, sin_b)
        o_a = _delta_mixer(proj, gates, _pad_lanes(dn_a_log[l]), _pad_lanes(dn_dt_bias[l]),
                           row(dn_out_norm[l]))
        outs, lses = zip(*[_dilated_attention_pass(src, d)
                           for src, d in zip([proj] + grouped, DILATIONS)])
        x = _mix_ffn(o_a, outs, lses, x, w_out[l].astype(BF16), expand_mat, row(post_mix_norm[l]),
                     row(pre_ffn_norm[l]), ffn_up[l].astype(BF16), ffn_conv[l].astype(F32) * val_half,
                     row(ffn_conv_bias[l]) * val_half, ffn_down[l].astype(BF16),
                     row(post_ffn_norm[l]))
    return x
```

```python
import functools
import math

import jax
import jax.numpy as jnp
from jax import lax
from jax.experimental import pallas as pl
from jax.experimental.pallas import tpu as pltpu

F32 = jnp.float32
BF16 = jnp.bfloat16

DN_HEADS = 4
DN_HEAD_DIM = 128
DN_WIDTH = DN_HEADS * DN_HEAD_DIM
DN_CONV = 4
SWA_HEADS = 8
SWA_HEAD_DIM = 64
SWA_WIDTH = SWA_HEADS * SWA_HEAD_DIM
DILATIONS = (1, 4, 16)
DIL_STEP = 4
ATTN_BLOCK = 128
ATTN_SPAN = 128
ROPE_THETA = 500000.0
ROPE_DIM = SWA_HEAD_DIM // 4
ROPE_HALF = ROPE_DIM // 2
FFN_CONV = 3
NORM_EPS = 1e-6

LANES = 128
SUBLANES = 8
BF16_ROWS = 16
VMEM_LIMIT_BYTES = 56 * 1024 * 1024

TOKEN_TILE = 512
DELTA_BLOCK = 512
DELTA_CHUNK = 128
INV_BASE = 8
PROJ_TILE = 512
N_PROJ_TILES = 7
ATTN_TILE0 = 4
GATE_LANES = 2 * LANES
ATTN_ROWS_PER_STEP = 2048


def _params(n_axes):
    return pltpu.CompilerParams(
        dimension_semantics=("arbitrary",) * n_axes, vmem_limit_bytes=VMEM_LIMIT_BYTES)


def _sigmoid(x):
    return 1.0 / (1.0 + jnp.exp(-x))


def _rms_scale(x):
    return lax.rsqrt(jnp.mean(x * x, axis=-1, keepdims=True) + NORM_EPS)


def _split3(x):
    hi = x.astype(BF16)
    r = x - hi.astype(F32)
    mid = r.astype(BF16)
    lo = (r - mid.astype(F32)).astype(BF16)
    return hi, mid, lo


def _in_proj_kernel(x_ref, xh_ref, nw_ref, w_ref, wg_ref, cw_ref, cos_ref, sa_ref, sb_ref,
                    proj_ref, gate_ref, *rest, tm):
    res_refs, (h_scr, conv_scr, att_scr, mid_scr) = rest[:-4], rest[-4:]
    groups_per_tile = PROJ_TILE // LANES
    halo = BF16_ROWS
    x = x_ref[...]
    h_scr[halo:, :] = (x * _rms_scale(x) * nw_ref[...]).astype(BF16)
    xh = xh_ref[...]
    h_scr[0:halo, :] = jnp.where(pl.program_id(0) == 0, 0.0,
                                 xh * _rms_scale(xh) * nw_ref[...]).astype(BF16)
    gate_ref[...] = jnp.dot(h_scr[halo:, :], wg_ref[...], preferred_element_type=F32)

    def conv_silu_norm(j, cols):
        conv_scr[j] = jnp.dot(h_scr[...], w_ref[:, cols], preferred_element_type=F32)
        acc = conv_scr[j, halo:halo + tm, :] * cw_ref[DN_CONV - 1:DN_CONV, cols]
        for t in range(DN_CONV - 1):
            off = halo - (DN_CONV - 1) + t
            acc = acc + conv_scr[j, off:off + tm, :] * cw_ref[t:t + 1, cols]
        y = acc * _sigmoid(acc)
        if j == 2:
            return y
        scale = DN_HEAD_DIM ** -0.5 if j == 0 else 1.0
        heads = []
        for hd in range(DN_HEADS):
            t = y[:, hd * DN_HEAD_DIM:(hd + 1) * DN_HEAD_DIM]
            heads.append(t * (lax.rsqrt(jnp.sum(t * t, axis=-1, keepdims=True) + NORM_EPS) * scale))
        return jnp.concatenate(heads, axis=1)

    def regroup(planes):
        src_scr, rows_per_group = att_scr, tm
        for level, res_ref in enumerate(res_refs):
            n_out = rows_per_group // DIL_STEP
            for grp in range(DIL_STEP ** level):
                for step in range(DIL_STEP):
                    r = step * DIL_STEP ** level + grp
                    start = grp * rows_per_group + step
                    for g in planes:
                        t = src_scr[g, pl.ds(start, n_out, stride=DIL_STEP), :]
                        res_ref[r, :, g * LANES:(g + 1) * LANES] = t.astype(BF16)
                        if level + 1 < len(res_refs):
                            mid_scr[g, r * n_out:(r + 1) * n_out, :] = t
            src_scr, rows_per_group = mid_scr, n_out

    tile_order = list(range(ATTN_TILE0, N_PROJ_TILES)) + list(range(ATTN_TILE0))
    for j in tile_order:
        if j < N_PROJ_TILES - ATTN_TILE0:
            regroup(range(j * groups_per_tile, (j + 1) * groups_per_tile))
        cols = slice(j * PROJ_TILE, (j + 1) * PROJ_TILE)
        if j < 3:
            proj_ref[:, cols] = conv_silu_norm(j, cols).astype(BF16)
            continue
        y = jnp.dot(h_scr[halo:, :], w_ref[:, cols], preferred_element_type=F32)
        if j in (4, 5):
            if j == 4:
                y = y * (SWA_HEAD_DIM ** -0.5)
            y = (y * cos_ref[...]
                 + pltpu.roll(y, PROJ_TILE - ROPE_HALF, 1) * sa_ref[...]
                 + pltpu.roll(y, ROPE_HALF, 1) * sb_ref[...])
        proj_ref[:, cols] = y.astype(BF16)
        if j >= ATTN_TILE0:
            for g in range(groups_per_tile):
                att_scr[(j - ATTN_TILE0) * groups_per_tile + g] = y[:, g * LANES:(g + 1) * LANES]


def _in_proj(x, norm_w, w_main, w_gate, conv_w, cos_f, sin_a, sin_b):
    B, S, D = x.shape
    tm = min(TOKEN_TILE, S)
    halo = BF16_ROWS
    hb = tm // halo
    n_cols = N_PROJ_TILES * PROJ_TILE
    att_cols = 3 * SWA_WIDTH
    res_specs = [pl.BlockSpec((None, d, tm // d, att_cols), lambda s, b: (b, 0, s, 0))
                 for d in DILATIONS[1:]]
    res_shapes = [jax.ShapeDtypeStruct((B, d, S // d, att_cols), BF16) for d in DILATIONS[1:]]
    return pl.pallas_call(
        functools.partial(_in_proj_kernel, tm=tm),
        grid=(S // tm, B),
        in_specs=[
            pl.BlockSpec((None, tm, D), lambda s, b: (b, s, 0)),
            pl.BlockSpec((None, halo, D), lambda s, b: (b, jnp.maximum(s * hb - 1, 0), 0)),
            pl.BlockSpec((1, D), lambda s, b: (0, 0)),
            pl.BlockSpec((D, n_cols), lambda s, b: (0, 0)),
            pl.BlockSpec((D, GATE_LANES), lambda s, b: (0, 0)),
            pl.BlockSpec((DN_CONV, 3 * DN_WIDTH), lambda s, b: (0, 0)),
            pl.BlockSpec((tm, PROJ_TILE), lambda s, b: (s, 0)),
            pl.BlockSpec((tm, PROJ_TILE), lambda s, b: (s, 0)),
            pl.BlockSpec((tm, PROJ_TILE), lambda s, b: (s, 0)),
        ],
        out_specs=[
            pl.BlockSpec((None, tm, n_cols), lambda s, b: (b, s, 0)),
            pl.BlockSpec((None, tm, GATE_LANES), lambda s, b: (b, s, 0)),
        ] + res_specs,
        out_shape=[
            jax.ShapeDtypeStruct((B, S, n_cols), BF16),
            jax.ShapeDtypeStruct((B, S, GATE_LANES), F32),
        ] + res_shapes,
        scratch_shapes=[pltpu.VMEM((tm + halo, D), BF16), pltpu.VMEM((3, tm + halo, PROJ_TILE), F32)]
        + [pltpu.VMEM((att_cols // LANES, tm, LANES), F32)] * 2,
        compiler_params=_params(2),
        name="in_proj",
    )(x, x, norm_w, w_main, w_gate, conv_w, cos_f, sin_a, sin_b)


def _delta_kernel(qkv_ref, z_ref, gate_ref, alog_ref, dtb_ref, onw_ref, o_ref, state_scr,
                  *, tb, chunk):
    @pl.when(pl.program_id(1) == 0)
    def _():
        state_scr[...] = jnp.zeros_like(state_scr)

    gates = gate_ref[...]
    beta_all = _sigmoid(gates[:, :LANES])
    a_in = gates[:, LANES:] + dtb_ref[...]
    softplus = jnp.maximum(a_in, 0.0) + jnp.log(1.0 + jnp.exp(-jnp.abs(a_in)))
    g_all = -jnp.exp(alog_ref[...]) * softplus
    g_parts = _split3(g_all)

    r_i = lax.broadcasted_iota(jnp.int32, (chunk, chunk), 0)
    c_i = lax.broadcasted_iota(jnp.int32, (chunk, chunk), 1)
    incl = r_i >= c_i
    strict = r_i > c_i
    tri = jnp.where(incl, 1.0, 0.0).astype(BF16)
    eye = jnp.where(r_i == c_i, 1.0, 0.0).astype(F32)

    def same_block(log_size):
        return (r_i >> log_size) == (c_i >> log_size)

    log_s = int(math.log2(INV_BASE))
    level_masks = [same_block(log_s)]
    while (1 << log_s) < chunk:
        level_masks.append(same_block(log_s + 1) & (((r_i >> log_s) & 1) == 1)
                           & (((c_i >> log_s) & 1) == 0))
        log_s += 1

    def head_cols(part, h):
        return slice(part * DN_WIDTH + h * DN_HEAD_DIM, part * DN_WIDTH + (h + 1) * DN_HEAD_DIM)

    def mm(a, b):
        return jnp.dot(a.astype(BF16), b.astype(BF16), preferred_element_type=F32)

    n_chunks = tb // chunk
    probs = [(c, h) for c in range(n_chunks) for h in range(DN_HEADS)]
    row_of = lambda c: slice(c * chunk, (c + 1) * chunk)

    Gs = [sum(jnp.dot(tri, p[row_of(c)], preferred_element_type=F32) for p in g_parts)
          for c in range(n_chunks)]
    G_ts = [G.T for G in Gs]

    q_l = [qkv_ref[row_of(c), head_cols(0, h)].astype(F32) for c, h in probs]
    k_l = [qkv_ref[row_of(c), head_cols(1, h)].astype(F32) for c, h in probs]
    v_l = [qkv_ref[row_of(c), head_cols(2, h)].astype(F32) for c, h in probs]
    gcol_l = [Gs[c][:, h:h + 1] for c, h in probs]
    grow_l = [G_ts[c][h:h + 1, :] for c, h in probs]
    glast_l = [g[:, chunk - 1:chunk] for g in grow_l]
    bcol_l = [beta_all[row_of(c), h:h + 1] for c, h in probs]
    gamma_l = [jnp.exp(jnp.where(incl, gc - gr, -jnp.inf)) for gc, gr in zip(gcol_l, grow_l)]
    eg_l = [jnp.exp(g) for g in gcol_l]
    kb_l = [k * b for k, b in zip(k_l, bcol_l)]
    kkqk_l = [lax.dot_general(jnp.concatenate([kb, q], axis=0).astype(BF16), k.astype(BF16),
                              (((1,), (1,)), ((), ())), preferred_element_type=F32)
              for kb, q, k in zip(kb_l, q_l, k_l)]
    a_l = [jnp.where(strict, kk[:chunk] * gm, 0.0) for kk, gm in zip(kkqk_l, gamma_l)]
    qk_l = [kk[chunk:] * gm for kk, gm in zip(kkqk_l, gamma_l)]

    p_l = [jnp.where(level_masks[0], -a, 0.0) for a in a_l]
    t_l = [eye + x for x in p_l]
    for _ in range(int(math.log2(INV_BASE)) - 1):
        p_l = [mm(p, p) for p in p_l]
        t_l = [t + mm(p, t) for p, t in zip(p_l, t_l)]
    for lvl_mask in level_masks[1:]:
        tb_l = [t.astype(BF16) for t in t_l]
        ld_l = [jnp.dot(jnp.where(lvl_mask, a, 0.0).astype(BF16), t_b, preferred_element_type=F32)
                for a, t_b in zip(a_l, tb_l)]
        t_l = [t - jnp.dot(t_b, ld.astype(BF16), preferred_element_type=F32)
               for t, t_b, ld in zip(t_l, tb_l, ld_l)]
    sol_l = [mm(t, jnp.concatenate([v * b, kb * eg], axis=1))
             for t, v, b, kb, eg in zip(t_l, v_l, bcol_l, kb_l, eg_l)]
    wq_l = [jnp.concatenate([sol[:, DN_HEAD_DIM:], q * eg], axis=0).astype(BF16)
            for sol, q, eg in zip(sol_l, q_l, eg_l)]
    kdt_l = [(k * jnp.exp(gl - gc)).T.astype(BF16) for k, gl, gc in zip(k_l, glast_l, gcol_l)]

    states = [state_scr[h] for h in range(DN_HEADS)]
    for c in range(n_chunks):
        ids = [c * DN_HEADS + h for h in range(DN_HEADS)]
        ws_l = [jnp.dot(wq_l[i], states[h].astype(BF16), preferred_element_type=F32)
                for h, i in enumerate(ids)]
        vnew_l = [(sol_l[i][:, :DN_HEAD_DIM] - ws[:chunk]).astype(BF16) for i, ws in zip(ids, ws_l)]
        o_l = [ws[chunk:] + jnp.dot(qk_l[i].astype(BF16), vn, preferred_element_type=F32)
               for i, ws, vn in zip(ids, ws_l, vnew_l)]
        states = [states[h] * jnp.exp(glast_l[i]) + jnp.dot(kdt_l[i], vn, preferred_element_type=F32)
                  for (h, i), vn in zip(enumerate(ids), vnew_l)]
        for h, o in enumerate(o_l):
            cols = slice(h * DN_HEAD_DIM, (h + 1) * DN_HEAD_DIM)
            zf = z_ref[row_of(c), cols].astype(F32)
            o = o * _rms_scale(o) * onw_ref[...] * (zf * _sigmoid(zf))
            o_ref[row_of(c), cols] = o.astype(BF16)
    for h in range(DN_HEADS):
        state_scr[h] = states[h]


def _delta_mixer(proj, gates, a_log, dt_bias, out_norm_w):
    B, S, _ = proj.shape
    tb = min(DELTA_BLOCK, S)
    chunk = min(DELTA_CHUNK, tb)
    kern = functools.partial(_delta_kernel, tb=tb, chunk=chunk)
    return pl.pallas_call(
        kern,
        grid=(B, S // tb),
        in_specs=[
            pl.BlockSpec((None, tb, 3 * DN_WIDTH), lambda b, s: (b, s, 0)),
            pl.BlockSpec((None, tb, DN_WIDTH), lambda b, s: (b, s, 3)),
            pl.BlockSpec((None, tb, GATE_LANES), lambda b, s: (b, s, 0)),
            pl.BlockSpec((1, LANES), lambda b, s: (0, 0)),
            pl.BlockSpec((1, LANES), lambda b, s: (0, 0)),
            pl.BlockSpec((1, DN_HEAD_DIM), lambda b, s: (0, 0)),
        ],
        out_specs=pl.BlockSpec((None, tb, DN_WIDTH), lambda b, s: (b, s, 0)),
        out_shape=jax.ShapeDtypeStruct((B, S, DN_WIDTH), BF16),
        scratch_shapes=[pltpu.VMEM((DN_HEADS, DN_HEAD_DIM, DN_HEAD_DIM), F32)],
        compiler_params=_params(2),
        name="delta_mixer",
    )(proj, proj, gates, a_log, dt_bias, out_norm_w)


def _attn_kernel(q_ref, kp_ref, kc_ref, vp_ref, vc_ref, o_ref, lse_ref, *, bq, n_res):
    blk = ATTN_BLOCK
    q_i = lax.broadcasted_iota(jnp.int32, (blk, 2 * blk), 0)
    k_i = lax.broadcasted_iota(jnp.int32, (blk, 2 * blk), 1)
    band = (k_i >= q_i) & (k_i <= q_i + ATTN_SPAN)
    first_lo = jnp.where(pl.program_id(2) == 0, blk, 0)
    band_first = band & (k_i >= first_lo)
    lane = lax.broadcasted_iota(jnp.int32, (blk, LANES), 1)
    low_half = lane < SWA_HEAD_DIM

    col_of = lambda h: slice((h // 2) * LANES, (h // 2 + 1) * LANES)
    row_of = lambda s: slice(s * blk, (s + 1) * blk)

    for r in range(n_res):
        for s in range(bq // blk):
            if s == 0:
                k = jnp.concatenate([kp_ref[r], kc_ref[r, 0:blk, :]], axis=0)
                v = jnp.concatenate([vp_ref[r], vc_ref[r, 0:blk, :]], axis=0)
                valid = band_first
            else:
                window = slice((s - 1) * blk, (s + 1) * blk)
                k, v, valid = kc_ref[r, window, :], vc_ref[r, window, :], band
            heads = range(SWA_HEADS)
            qh_l = [jnp.where(low_half if h % 2 == 0 else jnp.logical_not(low_half),
                              q_ref[r, row_of(s), col_of(h)], jnp.zeros((blk, LANES), BF16))
                    for h in heads]
            sc_l = [lax.dot_general(qh, k[:, col_of(h)], (((1,), (1,)), ((), ())),
                                    preferred_element_type=F32) for h, qh in zip(heads, qh_l)]
            sc_l = [jnp.where(valid, sc, -jnp.inf) for sc in sc_l]
            m_l = [jnp.max(sc, axis=-1, keepdims=True) for sc in sc_l]
            p_l = [jnp.exp(sc - m) for sc, m in zip(sc_l, m_l)]
            l_l = [jnp.sum(p, axis=-1, keepdims=True) for p in p_l]
            pv_l = [jnp.dot(p.astype(BF16), v[:, col_of(h)], preferred_element_type=F32)
                    for h, p in zip(heads, p_l)]
            o_l = [pv / l for pv, l in zip(pv_l, l_l)]
            lse_acc = jnp.zeros((blk, LANES), F32)
            for h in heads:
                lse_acc = jnp.where(lane == h, m_l[h] + jnp.log(l_l[h]), lse_acc)
            lse_ref[r, row_of(s), :] = lse_acc
            for h in range(0, SWA_HEADS, 2):
                o_ref[r, row_of(s), col_of(h)] = jnp.where(low_half, o_l[h], o_l[h + 1]).astype(BF16)


def _dilated_attention_pass(src, dil):
    blk = ATTN_BLOCK
    if dil == 1:
        B, L, _ = src.shape
        tile0 = ATTN_TILE0
        src = src.reshape(B, 1, L, src.shape[-1])
    else:
        B, _, L, _ = src.shape
        tile0 = 0
    bq = min(ATTN_ROWS_PER_STEP, L)
    n_res = min(dil, ATTN_ROWS_PER_STEP // bq)
    sub = bq // blk
    kern = functools.partial(_attn_kernel, bq=bq, n_res=n_res)

    def cur(col):
        return pl.BlockSpec((None, n_res, bq, PROJ_TILE), lambda b, r, i: (b, r, i, tile0 + col))

    def prev(col):
        return pl.BlockSpec((None, n_res, blk, PROJ_TILE),
                            lambda b, r, i: (b, r, jnp.maximum(i * sub - 1, 0), tile0 + col))

    return pl.pallas_call(
        kern,
        grid=(B, dil // n_res, L // bq),
        in_specs=[cur(0), prev(1), cur(1), prev(2), cur(2)],
        out_specs=[
            pl.BlockSpec((None, n_res, bq, SWA_WIDTH), lambda b, r, i: (b, r, i, 0)),
            pl.BlockSpec((None, n_res, bq, LANES), lambda b, r, i: (b, r, i, 0)),
        ],
        out_shape=[
            jax.ShapeDtypeStruct((B, dil, L, SWA_WIDTH), BF16),
            jax.ShapeDtypeStruct((B, dil, L, LANES), F32),
        ],
        compiler_params=_params(3),
        name=f"dilated_attn_d{dil}",
    )(src, src, src, src, src)


def _mix_residual(oa_ref, o_refs, l_refs, x, w_ref, e_ref, nw_ref, o_scr, l_scr, tm):
    n_planes = SWA_WIDTH // LANES
    tmp_slot = len(DILATIONS) - 1

    def token_order(o_ref, l_ref, level):
        if level == 0:
            return o_ref[0].astype(F32), l_ref[0]

        def from_input(r):
            o_res = o_ref[r].astype(F32)
            return [o_res[:, g * LANES:(g + 1) * LANES] for g in range(n_planes)], l_ref[r]

        get, rows = from_input, tm // DIL_STEP ** level
        for lvl in range(level, 0, -1):
            dst = level - 1 if lvl == 1 else tmp_slot
            for grp in range(DIL_STEP ** (lvl - 1)):
                for step in range(DIL_STEP):
                    planes, l_res = get(step * DIL_STEP ** (lvl - 1) + grp)
                    dst_rows = pl.ds(grp * rows * DIL_STEP + step, rows, stride=DIL_STEP)
                    for g in range(n_planes):
                        o_scr[dst * n_planes + g, dst_rows, :] = planes[g]
                    l_scr[dst, dst_rows, :] = l_res
            rows *= DIL_STEP

            def get(r, src=dst, n=rows):
                blk = slice(r * n, (r + 1) * n)
                return [o_scr[src * n_planes + g, blk, :] for g in range(n_planes)], l_scr[src, blk, :]

        planes, l_tok = get(0)
        return jnp.concatenate(planes, axis=1), l_tok

    pairs = [token_order(o, l, lvl) for lvl, (o, l) in enumerate(zip(o_refs, l_refs))]
    lses = [l for _, l in pairs]
    m = jnp.maximum(jnp.maximum(lses[0], lses[1]), lses[2])
    es = [jnp.exp(l - m) for l in lses]
    den = es[0] + es[1] + es[2]

    def expand(wt):
        hi = wt.astype(BF16)
        lo = (wt - hi.astype(F32)).astype(BF16)
        return jnp.dot(jnp.concatenate([hi, lo], axis=1), e_ref[...], preferred_element_type=F32)

    ob = None
    for e, (o, _) in zip(es, pairs):
        term = expand(e / den) * o
        ob = term if ob is None else ob + term
    y = (jnp.dot(oa_ref[...], w_ref[0:DN_WIDTH, :], preferred_element_type=F32)
         + jnp.dot(ob.astype(BF16), w_ref[DN_WIDTH:, :], preferred_element_type=F32))
    return x + y * _rms_scale(y) * nw_ref[...]


def _mix_ffn_kernel(oa_ref, o1_ref, o2_ref, o3_ref, l1_ref, l2_ref, l3_ref, x_ref, wo_ref, e_ref,
                    mnw_ref, nw_ref, wg_ref, wv_ref, cg_ref, cv_ref, bg_ref, bv_ref, wd_ref, pnw_ref,
                    out_ref, o_scr, l_scr, h_scr, u_scr, *, tm, tiles_per_seq):
    halo = BF16_ROWS

    @pl.when(pl.program_id(0) % tiles_per_seq == 0)
    def _():
        h_scr[0:halo, :] = jnp.zeros((halo, h_scr.shape[1]), BF16)

    x = _mix_residual(oa_ref, (o1_ref, o2_ref, o3_ref), (l1_ref, l2_ref, l3_ref), x_ref[...],
                      wo_ref, e_ref, mnw_ref, o_scr, l_scr, tm)
    h_scr[halo:, :] = (x * _rms_scale(x) * nw_ref[...]).astype(BF16)

    def conv(w_ref, c_ref, b_ref):
        u_scr[...] = jnp.dot(h_scr[...], w_ref[...], preferred_element_type=F32)
        acc = u_scr[halo:halo + tm, :] * c_ref[FFN_CONV - 1:FFN_CONV, :]
        for t in range(FFN_CONV - 1):
            off = halo - (FFN_CONV - 1) + t
            acc = acc + u_scr[off:off + tm, :] * c_ref[t:t + 1, :]
        return acc + b_ref[...]

    gate = conv(wg_ref, cg_ref, bg_ref)
    c = math.sqrt(2.0 / math.pi)
    inner = gate * (c + (0.044715 * c) * (gate * gate))
    act = (gate * (1.0 + jnp.tanh(inner))) * conv(wv_ref, cv_ref, bv_ref)
    y = jnp.dot(act.astype(BF16), wd_ref[...], preferred_element_type=F32)
    out_ref[...] = x + y * _rms_scale(y) * pnw_ref[...]
    h_scr[0:halo, :] = h_scr[tm:tm + halo, :]


def _mix_ffn(o_a, outs, lses, x, w_out, expand_mat, mix_norm_w, pre_w, w_up, conv_w, conv_b, w_down,
             post_w):
    B, S, D = x.shape
    d_ff = w_down.shape[0]
    tm = min(TOKEN_TILE, S)
    tps = S // tm

    def tok(width):
        return pl.BlockSpec((None, tm, width), lambda i: (i // tps, i % tps, 0))

    def grouped(width):
        return [pl.BlockSpec((None, d, tm // d, width), lambda i: (i // tps, 0, i % tps, 0))
                for d in DILATIONS]

    def const(shape, col=0):
        return pl.BlockSpec(shape, lambda i: (0, col))

    n_slots = len(DILATIONS)
    return pl.pallas_call(
        functools.partial(_mix_ffn_kernel, tm=tm, tiles_per_seq=tps),
        grid=(B * tps,),
        in_specs=[tok(DN_WIDTH)] + grouped(SWA_WIDTH) + grouped(LANES) + [
            tok(D),
            const((DN_WIDTH + SWA_WIDTH, D)),
            const((2 * LANES, SWA_WIDTH)),
            const((1, D)),
            const((1, D)),
            const((D, d_ff)), const((D, d_ff), 1),
            const((FFN_CONV, d_ff)), const((FFN_CONV, d_ff), 1),
            const((1, d_ff)), const((1, d_ff), 1),
            const((d_ff, D)),
            const((1, D))],
        out_specs=tok(D),
        out_shape=jax.ShapeDtypeStruct((B, S, D), F32),
        scratch_shapes=[pltpu.VMEM((n_slots * SWA_WIDTH // LANES, tm, LANES), F32),
                        pltpu.VMEM((n_slots, tm, LANES), F32),
                        pltpu.VMEM((tm + BF16_ROWS, D), BF16),
                        pltpu.VMEM((tm + BF16_ROWS, d_ff), F32)],
        compiler_params=_params(1),
        name="mix_ffn",
    )(o_a, *outs, *lses, x, w_out, expand_mat, mix_norm_w, pre_w, w_up, w_up, conv_w, conv_w,
      conv_b, conv_b, w_down, post_w)


def _rope_tables(S):
    pos = jnp.arange(S, dtype=F32)
    inv_freq = ROPE_THETA ** (-jnp.arange(0, ROPE_DIM, 2, dtype=F32) / ROPE_DIM)
    ang = pos[:, None] * inv_freq[None, :]
    cos, sin = jnp.cos(ang), jnp.sin(ang)
    pad = SWA_HEAD_DIM - ROPE_DIM
    ones = jnp.ones((S, pad), F32)
    zeros_h = jnp.zeros((S, ROPE_HALF), F32)
    zeros_p = jnp.zeros((S, pad), F32)
    cos_head = jnp.concatenate([cos, cos, ones], axis=1)
    sa_head = jnp.concatenate([-sin, zeros_h, zeros_p], axis=1)
    sb_head = jnp.concatenate([zeros_h, sin, zeros_p], axis=1)
    tile = lambda t: jnp.tile(t, (1, SWA_HEADS))
    return tile(cos_head), tile(sa_head), tile(sb_head)


def _pad_lanes(v):
    return jnp.zeros((1, LANES), F32).at[0, :v.shape[0]].set(v.astype(F32))


def kernel(x, pre_mix_norm, w_in, dn_conv, dn_a_log, dn_dt_bias, dn_out_norm, w_out, post_mix_norm,
           pre_ffn_norm, ffn_up, ffn_conv, ffn_conv_bias, ffn_down, post_ffn_norm):
    B, S, D = x.shape
    depth = w_in.shape[0]
    assert S % (max(DILATIONS) * ATTN_BLOCK) == 0 and S % TOKEN_TILE == 0
    cos_f, sin_a, sin_b = _rope_tables(S)
    qkvz = 4 * DN_WIDTH
    n_gate = 2 * DN_HEADS
    head_of_lane = jnp.arange(SWA_WIDTH) // SWA_HEAD_DIM
    expand_mat = (jnp.arange(2 * LANES)[:, None] % LANES == head_of_lane[None, :]).astype(BF16)
    row = lambda v: v.reshape(1, -1).astype(F32)
    d_ff = ffn_down.shape[1]
    val_half = jnp.where(jnp.arange(2 * d_ff) < d_ff, 1.0, 0.5).astype(F32)[None, :]

    for l in range(depth):
        w_l = w_in[l]
        w_main = jnp.concatenate([w_l[:, :qkvz], w_l[:, qkvz + n_gate:]], axis=1).astype(BF16)
        w_gate = jnp.zeros((D, GATE_LANES), F32)
        w_gate = w_gate.at[:, :DN_HEADS].set(w_l[:, qkvz:qkvz + DN_HEADS])
        w_gate = w_gate.at[:, LANES:LANES + DN_HEADS].set(w_l[:, qkvz + DN_HEADS:qkvz + n_gate])
        proj, gates, *grouped = _in_proj(x, row(pre_mix_norm[l]), w_main, w_gate.astype(BF16),
                                         dn_conv[l].astype(F32), cos_f, sin_a, sin_b)
        o_a = _delta_mixer(proj, gates, _pad_lanes(dn_a_log[l]), _pad_lanes(dn_dt_bias[l]),
                           row(dn_out_norm[l]))
        outs, lses = zip(*[_dilated_attention_pass(src, d)
                           for src, d in zip([proj] + grouped, DILATIONS)])
        x = _mix_ffn(o_a, outs, lses, x, w_out[l].astype(BF16), expand_mat, row(post_mix_norm[l]),
                     row(pre_ffn_norm[l]), ffn_up[l].astype(BF16), ffn_conv[l].astype(F32) * val_half,
                     row(ffn_conv_bias[l]) * val_half, ffn_down[l].astype(BF16),
                     row(post_ffn_norm[l]))
    return x
```
